```python
import math
import jax, jax.numpy as jnp
from jax import lax
import numpy as np

D_MODEL = 1024
BATCH = 2
SEQ = 8192
DEPTH = 2
DEC_BATCH = 32
DEC_SEQ = 4
PAST_LEN = 8192
PAGE_SIZE = 128

HEAD_DIM = 64
N_HEADS_A = 6
N_HEADS_B = 6
D_A = N_HEADS_A * HEAD_DIM
D_B = N_HEADS_B * HEAD_DIM
D_C = D_MODEL - D_A - D_B
DIFF_DIM = HEAD_DIM // 2
IDX_HEADS = 4
IDX_DIM = 64
TOPK_MAX = 256
CONV_WIDTH = 3
NUM_BUCKETS = 32
MAX_DISTANCE = 128
D_FF = 2816
N_EXPERTS = 8
TOP_K_EXPERTS = 2
Q_BLOCK = 128
EPS = 1e-6
COL_SIZES = (D_A, D_A, D_A, IDX_HEADS * IDX_DIM, IDX_DIM, IDX_HEADS, D_B, D_B, D_B, D_C, D_C, D_C)
P_TOTAL = sum(COL_SIZES)

kernel_name = 'hybrid_dsa_diff_conv_decoder_step'


def rmsnorm(x, g):
    xf = x.astype(jnp.float32)
    y = xf * lax.rsqrt(jnp.mean(xf * xf, axis=-1, keepdims=True) + EPS)
    return (y * g.astype(jnp.float32)).astype(x.dtype)


def t5_bucket(rel):
    n = jnp.maximum(rel, 0)
    max_exact = NUM_BUCKETS // 2
    nf = jnp.maximum(n, max_exact).astype(jnp.float32)
    large = max_exact + (jnp.log(nf / max_exact) / math.log(MAX_DISTANCE / max_exact)
                         * (NUM_BUCKETS - max_exact)).astype(jnp.int32)
    return jnp.where(n < max_exact, n, jnp.minimum(large, NUM_BUCKETS - 1))


def map_query_blocks(fn, q_pos, *q_args):
    tq = q_pos.shape[0]
    if tq <= Q_BLOCK or tq % Q_BLOCK != 0:
        return fn(q_pos, *q_args)
    nb = tq // Q_BLOCK
    def to_blocks(a):
        return jnp.swapaxes(a.reshape(a.shape[0], nb, Q_BLOCK, *a.shape[2:]), 0, 1)
    xs = (q_pos.reshape(nb, Q_BLOCK),) + tuple(to_blocks(a) for a in q_args)
    out = lax.map(lambda args: fn(*args), xs)
    out = jnp.swapaxes(out, 0, 1)
    return out.reshape(out.shape[0], tq, *out.shape[3:])


def dsa_attention(q, k, v, qi, ki, wi, q_pos, k_pos, bias_tab, topk):
    def block(qp, qb, qib, wib):
        s = jnp.einsum('bqhd,bkd->bqhk', qib, ki) * IDX_DIM ** -0.5
        score = jnp.einsum('bqh,bqhk->bqk', wib, jax.nn.relu(s)).astype(jnp.float32) * IDX_HEADS ** -0.5
        causal = k_pos[None, :] <= qp[:, None]
        score = jnp.where(causal[None], score, -jnp.inf)
        top_val, sel = lax.top_k(score, topk)
        valid = jnp.isfinite(top_val)
        k_sel = jax.vmap(lambda kk, ss: kk[ss])(k, sel)
        v_sel = jax.vmap(lambda vv, ss: vv[ss])(v, sel)
        bias = bias_tab[t5_bucket(qp[None, :, None] - k_pos[sel])]
        logits = jnp.einsum('bqhd,bqkhd->bqhk', qb, k_sel).astype(jnp.float32) * HEAD_DIM ** -0.5
        logits = logits + jnp.moveaxis(bias, -1, 2).astype(jnp.float32)
        logits = jnp.where(valid[:, :, None, :], logits, -jnp.inf)
        p = jax.nn.softmax(logits, axis=-1)
        return jnp.einsum('bqhk,bqkhd->bqhd', p.astype(v.dtype), v_sel)
    return map_query_blocks(block, q_pos, q, qi, wi)


def diff_attention(q, k, v, q_pos, k_pos, lam, lam_init, sub_g, bias_tab):
    def block(qp, qb):
        logits = jnp.einsum('bqhcd,bkhcd->bchqk', qb, k).astype(jnp.float32) * DIFF_DIM ** -0.5
        bias = jnp.moveaxis(bias_tab[t5_bucket(qp[:, None] - k_pos[None, :])], -1, 0)
        causal = k_pos[None, :] <= qp[:, None]
        logits = jnp.where(causal, logits + bias.astype(jnp.float32), -jnp.inf)
        p = jax.nn.softmax(logits, axis=-1)
        a = p[:, 0] - lam * p[:, 1]
        o = jnp.einsum('bhqk,bkhd->bqhd', a.astype(v.dtype), v)
        return rmsnorm(o, sub_g) * (1.0 - lam_init)
    return map_query_blocks(block, q_pos, q)


def short_conv_mixer(h_c, g_b, g_c, prev, w):
    u = g_c * h_c
    ext = jnp.concatenate([prev.astype(u.dtype), u], axis=1)
    t = u.shape[1]
    y = w[0] * ext[:, 0:t]
    for j in range(1, CONV_WIDTH):
        y = y + w[j] * ext[:, j:j + t]
    return g_b * y, ext[:, -(CONV_WIDTH - 1):]


def swiglu(h, w1, w3, w2):
    return (jax.nn.silu(h @ w1) * (h @ w3)) @ w2


def moe_ffn(h, w_r, b_r, w1, w3, w2):
    logits = (h @ w_r + b_r).astype(jnp.float32)
    top_val, top_idx = lax.top_k(logits, TOP_K_EXPERTS)
    gates = jax.nn.softmax(top_val, axis=-1)
    dense_gate = jnp.sum(jax.nn.one_hot(top_idx, N_EXPERTS, dtype=jnp.float32) * gates[..., None], axis=-2)
    out = jnp.zeros_like(h)
    for e in range(N_EXPERTS):
        out = out + dense_gate[..., e:e + 1].astype(h.dtype) * swiglu(h, w1[e], w3[e], w2[e])
    return out


def mixer_sublayer(l, x, q_pos, k_pos, past, topk, mix_w):
    w_in, w_out, norm_mix, rel_bias, lam_q1, lam_k1, lam_q2, lam_k2, subln, conv_w = mix_w
    bsz, t = x.shape[:2]
    h = rmsnorm(x, norm_mix[l])
    proj = h @ w_in[l]
    q_a, k_a, v_a, q_i, k_i, w_i, q_b, k_b, v_b, h_c, g_b, g_c = jnp.split(
        proj, np.cumsum(COL_SIZES)[:-1].tolist(), axis=-1)
    q_a = q_a.reshape(bsz, t, N_HEADS_A, HEAD_DIM)
    k_a = k_a.reshape(bsz, t, N_HEADS_A, HEAD_DIM)
    v_a = v_a.reshape(bsz, t, N_HEADS_A, HEAD_DIM)
    q_i = q_i.reshape(bsz, t, IDX_HEADS, IDX_DIM)
    q_b = q_b.reshape(bsz, t, N_HEADS_B, HEAD_DIM)
    k_b = k_b.reshape(bsz, t, N_HEADS_B, HEAD_DIM)
    v_b = v_b.reshape(bsz, t, N_HEADS_B, HEAD_DIM)
    if past is None:
        conv_prev = jnp.zeros((bsz, CONV_WIDTH - 1, D_C), x.dtype)
        ka_all, va_all, ki_all, kb_all, vb_all = k_a, v_a, k_i, k_b, v_b
    else:
        pka, pva, pki, pkb, pvb, conv_prev = past
        cat = lambda p, n: jnp.concatenate([p.astype(n.dtype), n], axis=1)
        ka_all, va_all, ki_all = cat(pka, k_a), cat(pva, v_a), cat(pki, k_i)
        kb_all, vb_all = cat(pkb, k_b), cat(pvb, v_b)
    o_a = dsa_attention(q_a, ka_all, va_all, q_i, ki_all, w_i, q_pos, k_pos,
                        rel_bias[:, :N_HEADS_A], topk)
    lam_init = 0.8 - 0.6 * math.exp(-0.3 * l)
    lam = (jnp.exp(jnp.sum(lam_q1[l].astype(jnp.float32) * lam_k1[l].astype(jnp.float32)))
           - jnp.exp(jnp.sum(lam_q2[l].astype(jnp.float32) * lam_k2[l].astype(jnp.float32))) + lam_init)
    split2 = lambda a: a.reshape(*a.shape[:-1], 2, DIFF_DIM)
    o_b = diff_attention(split2(q_b), split2(kb_all), vb_all, q_pos, k_pos, lam, lam_init,
                         subln[l], rel_bias[:, N_HEADS_A:])
    o_c, conv_state = short_conv_mixer(h_c, g_b, g_c, conv_prev, conv_w[l])
    mixed = jnp.concatenate([o_a.reshape(bsz, t, D_A), o_b.reshape(bsz, t, D_B), o_c], axis=-1)
    x = x + mixed @ w_out[l]
    return x, (k_a, v_a, k_i, k_b, v_b, conv_state)


def run_trunk(x, q_pos, k_pos, pasts, topk, mix_w, ffn_w, norm_final):
    norm_ffn, ffn_w1, ffn_w3, ffn_w2, moe_router, moe_router_b, moe_w1, moe_w3, moe_w2 = ffn_w
    new_rows = []
    for l in range(DEPTH):
        past = None if pasts is None else pasts(l)
        x, rows = mixer_sublayer(l, x, q_pos, k_pos, past, topk, mix_w)
        new_rows.append(rows)
        h = rmsnorm(x, norm_ffn[l])
        if l % 2 == 0:
            f = swiglu(h, ffn_w1[l // 2], ffn_w3[l // 2], ffn_w2[l // 2])
        else:
            f = moe_ffn(h, moe_router[l // 2], moe_router_b[l // 2],
                        moe_w1[l // 2], moe_w3[l // 2], moe_w2[l // 2])
        x = x + f
    y = rmsnorm(x, norm_final)
    stacked = tuple(jnp.stack([rows[i] for rows in new_rows]) for i in range(6))
    return y, stacked


def gather_pages(cache_l, page_table):
    g = cache_l[page_table]
    return g.reshape(g.shape[0], -1, *g.shape[3:])


def setup_inputs(seed: int = 0) -> dict:
    key = jax.random.key(seed)
    keys = iter(jax.random.split(key, 40))
    n_pages = PAST_LEN // PAGE_SIZE
    n_pool = (5 * DEC_BATCH * n_pages + 3) // 4
    n_dense = (DEPTH + 1) // 2
    n_moe = DEPTH // 2
    def nrm(shape, scale=1.0):
        return scale * jax.random.normal(next(keys), shape, jnp.float32)
    perm = jax.random.permutation(next(keys), n_pool)
    page_table = perm[: DEC_BATCH * n_pages].reshape(DEC_BATCH, n_pages).astype(jnp.int32)
    dsc = D_MODEL ** -0.5
    return {
        'x_prompt': nrm((BATCH, SEQ, D_MODEL)),
        'x_sample': nrm((DEC_BATCH, DEC_SEQ, D_MODEL)),
        'cache_a_k': nrm((DEPTH, n_pool, PAGE_SIZE, N_HEADS_A, HEAD_DIM)),
        'cache_a_v': nrm((DEPTH, n_pool, PAGE_SIZE, N_HEADS_A, HEAD_DIM)),
        'cache_a_kidx': nrm((DEPTH, n_pool, PAGE_SIZE, IDX_DIM)),
        'cache_b_k': nrm((DEPTH, n_pool, PAGE_SIZE, N_HEADS_B, HEAD_DIM)),
        'cache_b_v': nrm((DEPTH, n_pool, PAGE_SIZE, N_HEADS_B, HEAD_DIM)),
        'state_conv': nrm((DEPTH, DEC_BATCH, CONV_WIDTH - 1, D_C)),
        'page_table': page_table,
        'w_in': nrm((DEPTH, D_MODEL, P_TOTAL), dsc),
        'w_out': nrm((DEPTH, D_MODEL, D_MODEL), dsc),
        'norm_mix': 1.0 + nrm((DEPTH, D_MODEL), 0.02),
        'norm_ffn': 1.0 + nrm((DEPTH, D_MODEL), 0.02),
        'norm_final': 1.0 + nrm((D_MODEL,), 0.02),
        'rel_bias': nrm((NUM_BUCKETS, N_HEADS_A + N_HEADS_B), 0.5),
        'lam_q1': nrm((DEPTH, DIFF_DIM), 0.1),
        'lam_k1': nrm((DEPTH, DIFF_DIM), 0.1),
        'lam_q2': nrm((DEPTH, DIFF_DIM), 0.1),
        'lam_k2': nrm((DEPTH, DIFF_DIM), 0.1),
        'subln': 1.0 + nrm((DEPTH, HEAD_DIM), 0.02),
        'conv_w': nrm((DEPTH, CONV_WIDTH, D_C), CONV_WIDTH ** -0.5),
        'ffn_w1': nrm((n_dense, D_MODEL, D_FF), dsc),
        'ffn_w3': nrm((n_dense, D_MODEL, D_FF), dsc),
        'ffn_w2': nrm((n_dense, D_FF, D_MODEL), D_FF ** -0.5),
        'moe_router': nrm((n_moe, D_MODEL, N_EXPERTS), dsc),
        'moe_router_b': nrm((n_moe, N_EXPERTS), 0.01),
        'moe_w1': nrm((n_moe, N_EXPERTS, D_MODEL, D_FF), dsc),
        'moe_w3': nrm((n_moe, N_EXPERTS, D_MODEL, D_FF), dsc),
        'moe_w2': nrm((n_moe, N_EXPERTS, D_FF, D_MODEL), D_FF ** -0.5),
    }


def reference(x_prompt, x_sample, cache_a_k, cache_a_v, cache_a_kidx, cache_b_k, cache_b_v,
              state_conv, page_table, w_in, w_out, norm_mix, norm_ffn, norm_final, rel_bias,
              lam_q1, lam_k1, lam_q2, lam_k2, subln, conv_w, ffn_w1, ffn_w3, ffn_w2,
              moe_router, moe_router_b, moe_w1, moe_w3, moe_w2):
    mix_w = (w_in, w_out, norm_mix, rel_bias, lam_q1, lam_k1, lam_q2, lam_k2, subln, conv_w)
    ffn_w = (norm_ffn, ffn_w1, ffn_w3, ffn_w2, moe_router, moe_router_b, moe_w1, moe_w3, moe_w2)

    seq = x_prompt.shape[1]
    pos_p = jnp.arange(seq, dtype=jnp.int32)
    y_prompt, (p_a_k, p_a_v, p_a_kidx, p_b_k, p_b_v, p_conv) = run_trunk(
        x_prompt, pos_p, pos_p, None, min(TOPK_MAX, seq // 4), mix_w, ffn_w, norm_final)

    n_new = x_sample.shape[1]
    past_len = page_table.shape[1] * PAGE_SIZE
    k_pos_s = jnp.arange(past_len + n_new, dtype=jnp.int32)
    q_pos_s = k_pos_s[past_len:]
    def pasts(l):
        return (gather_pages(cache_a_k[l], page_table), gather_pages(cache_a_v[l], page_table),
                gather_pages(cache_a_kidx[l], page_table), gather_pages(cache_b_k[l], page_table),
                gather_pages(cache_b_v[l], page_table), state_conv[l])
    y_sample, (s_a_k, s_a_v, s_a_kidx, s_b_k, s_b_v, s_conv) = run_trunk(
        x_sample, q_pos_s, k_pos_s, pasts, min(TOPK_MAX, (past_len + n_new) // 4), mix_w, ffn_w, norm_final)

    return (y_prompt, y_sample, p_a_k, p_a_v, p_a_kidx, p_b_k, p_b_v, p_conv,
            s_a_k, s_a_v, s_a_kidx, s_b_k, s_b_v, s_conv)
```

```python
import functools
import math

import jax
import jax.numpy as jnp
import numpy as np
from jax import lax
from jax.experimental import pallas as pl
from jax.experimental.pallas import tpu as pltpu

F32 = jnp.float32
BF16 = jnp.bfloat16
I32 = jnp.int32

D_MODEL = 1024
HEAD_DIM = 64
N_HEADS_A = 6
N_HEADS_B = 6
D_A = N_HEADS_A * HEAD_DIM
D_B = N_HEADS_B * HEAD_DIM
D_C = D_MODEL - D_A - D_B
DIFF_DIM = HEAD_DIM // 2
IDX_HEADS = 4
IDX_DIM = 64
TOPK_MAX = 256
CONV_WIDTH = 3
NUM_BUCKETS = 32
MAX_DISTANCE = 128
N_EXPERTS = 8
EPS = 1e-6

LANES = 128
SUBLANES = 8
VMEM_LIMIT = 56 * 1024 * 1024
LOG2E = 1.4426950408889634
NEG = -1e30
INT_MIN = -2**31
BIG_IDX = 2**30

ATT_TILE = 256
ROW_BLOCK = 128
PAGES_PER_STEP = 8
N_PAIRS = D_A // LANES

C_QA, C_KA, C_VA, C_QI, C_KW, C_QB, C_KB, C_VB, C_HC, C_GB, C_GC, C_END = (
    0, 384, 768, 1152, 1408, 1536, 1920, 2304, 2688, 2944, 3200, 3456)


def _cparams(sem):
    return pltpu.CompilerParams(dimension_semantics=sem, vmem_limit_bytes=VMEM_LIMIT)


def _dot_t(a, b):
    return lax.dot_general(a, b, (((1,), (1,)), ((), ())), preferred_element_type=F32)


def _dot(a, b):
    return jnp.dot(a, b, preferred_element_type=F32)


def _rms(x, g):
    ms = jnp.mean(x * x, axis=-1, keepdims=True)
    return x * lax.rsqrt(ms + EPS) * g


def _proj_kernel(x_ref, g_ref, w_ref, qa_ref, ka_ref, kab_ref, va_ref, vab_ref, qi_ref, kw_ref,
                 kib_ref, qb_ref, kb_ref, kbb_ref, vb_ref, vbb_ref, u_ref, gb_ref):
    h = _rms(x_ref[...], g_ref[...]).astype(BF16)

    def mm(a, b):
        return _dot(h, w_ref[:, a:b])

    qa_ref[...] = (mm(C_QA, C_KA) * (HEAD_DIM ** -0.5 * LOG2E)).astype(BF16)
    ka = mm(C_KA, C_VA)
    ka_ref[...] = ka
    kab_ref[...] = ka.astype(BF16)
    va = mm(C_VA, C_QI)
    va_ref[...] = va
    vab_ref[...] = va.astype(BF16)
    qi_ref[...] = (mm(C_QI, C_KW) * IDX_DIM ** -0.5).astype(BF16)
    kw = mm(C_KW, C_QB)
    kw_ref[...] = kw
    kib_ref[...] = kw[:, :IDX_DIM].astype(BF16)
    qb_ref[...] = (mm(C_QB, C_KB) * (DIFF_DIM ** -0.5 * LOG2E)).astype(BF16)
    kb = mm(C_KB, C_VB)
    kb_ref[...] = kb
    kbb_ref[...] = kb.astype(BF16)
    vb = mm(C_VB, C_HC)
    vb_ref[...] = vb
    vbb_ref[...] = vb.astype(BF16)
    hc = mm(C_HC, C_GB)
    gb_ref[...] = mm(C_GB, C_GC)
    u_ref[...] = mm(C_GC, C_END) * hc


def _proj(x, g, w):
    n = x.shape[0]
    tm = min(256, n)
    assert n % tm == 0
    widths = dict(qa=(D_A, BF16), ka=(D_A, F32), kab=(D_A, BF16), va=(D_A, F32), vab=(D_A, BF16),
                  qi=(IDX_HEADS * IDX_DIM, BF16), kw=(LANES, F32), kib=(IDX_DIM, BF16),
                  qb=(D_B, BF16), kb=(D_B, F32), kbb=(D_B, BF16), vb=(D_B, F32), vbb=(D_B, BF16),
                  u=(D_C, F32), gb=(D_C, F32))
    names = list(widths)
    outs = pl.pallas_call(
        _proj_kernel,
        grid=(n // tm,),
        in_specs=[pl.BlockSpec((tm, D_MODEL), lambda i: (i, 0)),
                  pl.BlockSpec((1, D_MODEL), lambda i: (0, 0)),
                  pl.BlockSpec((D_MODEL, C_END), lambda i: (0, 0))],
        out_specs=[pl.BlockSpec((tm, widths[k][0]), lambda i: (i, 0)) for k in names],
        out_shape=[jax.ShapeDtypeStruct((n, widths[k][0]), widths[k][1]) for k in names],
        compiler_params=_cparams(("parallel",)),
        name="proj",
    )(x, g, w)
    return dict(zip(names, outs))


def _sortable_key(score):
    score = jnp.where(score == 0.0, 0.0, score)
    bits = pltpu.bitcast(score, I32)
    return bits ^ ((bits >> 31) & 0x7FFFFFFF)


def _topk_threshold(read_group, write_group, n_groups, rows, topk, idx_bits):
    lane = lax.broadcasted_iota(I32, (rows, LANES), 1)

    def count(pred):
        def body(g, acc):
            return acc + jnp.where(pred(read_group(g)), 1, 0)
        acc = lax.fori_loop(0, n_groups, body, jnp.zeros((rows, LANES), I32))
        return jnp.sum(acc.astype(F32), axis=1, keepdims=True)

    kf = float(topk)
    cnt = count(lambda blk: blk >= 0)
    t = jnp.where(cnt >= kf, 0, INT_MIN).astype(I32)

    def value_bit(i, t):
        cand = t | lax.shift_left(jnp.int32(1), 30 - i)
        cb = jnp.broadcast_to(cand, (rows, LANES))
        return jnp.where(count(lambda blk: blk >= cb) >= kf, cand, t)

    t = lax.fori_loop(0, 31, value_bit, t)
    t = jnp.maximum(t, INT_MIN + 1)
    tb = jnp.broadcast_to(t, (rows, LANES))

    def transform(g, c):
        blk = read_group(g)
        idx = g * LANES + lane
        write_group(g, jnp.where(blk > tb, -1, jnp.where(blk == tb, idx, BIG_IDX)))
        return c

    lax.fori_loop(0, n_groups, transform, 0)

    def index_bit(i, m):
        cand = m | lax.shift_left(jnp.int32(1), idx_bits - 1 - i)
        cb = jnp.broadcast_to(cand, (rows, LANES))
        return jnp.where(count(lambda blk: blk < cb) < kf, cand, m)

    return lax.fori_loop(0, idx_bits, index_bit, jnp.zeros((rows, 1), I32))


def _softmax_step(lg, vp, m_ref, l_ref, acc_ref):
    m_old = m_ref[...]
    m_new = jnp.maximum(m_old, jnp.max(lg, axis=1, keepdims=True))
    alpha = jnp.exp2(m_old - m_new)
    p = jnp.exp2(lg - m_new)
    l_ref[...] = alpha * l_ref[...] + jnp.sum(p, axis=1, keepdims=True)
    acc_ref[...] = alpha * acc_ref[...] + _dot(p.astype(BF16), vp)
    m_ref[...] = m_new


def _dsa_prompt_kernel(qi_ref, kw_ref, kib_ref, qa_ref, kab_ref, vab_ref, tiles_ref, oa_ref,
                       s_sc, thr_sc, qm_sc, m_sc, l_sc, acc_sc, *, topk, idx_bits):
    tq = ATT_TILE
    qi = pl.program_id(1)
    n_ch = qi + 1
    q0 = qi * tq

    w4 = kw_ref[:, IDX_DIM:IDX_DIM + IDX_HEADS] * IDX_HEADS ** -0.5
    row = lax.broadcasted_iota(I32, (tq, tq), 0)
    col = lax.broadcasted_iota(I32, (tq, tq), 1)

    def score_chunk(kc, c):
        k0 = pl.multiple_of(kc * tq, tq)
        kch = kib_ref[pl.ds(k0, tq), :]
        sc = jnp.zeros((tq, tq), F32)
        for h in range(IDX_HEADS):
            s = _dot_t(qi_ref[:, h * IDX_DIM:(h + 1) * IDX_DIM], kch)
            sc = sc + w4[:, h:h + 1] * jnp.maximum(s, 0.0)
        valid = (k0 + col) <= (q0 + row)
        s_sc[:, pl.ds(k0, tq)] = jnp.where(valid, _sortable_key(sc), INT_MIN)
        return c

    lax.fori_loop(0, n_ch, score_chunk, 0)

    n_groups = n_ch * (tq // LANES)

    def search(r, c):
        r0 = pl.multiple_of(r * ROW_BLOCK, ROW_BLOCK)

        def read_group(g):
            return s_sc[pl.ds(r0, ROW_BLOCK), pl.ds(pl.multiple_of(g * LANES, LANES), LANES)]

        def write_group(g, v):
            s_sc[pl.ds(r0, ROW_BLOCK), pl.ds(pl.multiple_of(g * LANES, LANES), LANES)] = v

        m = _topk_threshold(read_group, write_group, n_groups, ROW_BLOCK, topk, idx_bits)
        thr_sc[pl.ds(r0, ROW_BLOCK), :] = jnp.broadcast_to(m, (ROW_BLOCK, tq))
        return c

    lax.fori_loop(0, tq // ROW_BLOCK, search, 0)

    lane = lax.broadcasted_iota(I32, (tq, LANES), 1)
    for p in range(N_PAIRS):
        qp = qa_ref[:, p * LANES:(p + 1) * LANES]
        qm_sc[p, 0:tq, :] = jnp.where(lane < HEAD_DIM, qp, jnp.zeros_like(qp))
        qm_sc[p, tq:2 * tq, :] = jnp.where(lane >= HEAD_DIM, qp, jnp.zeros_like(qp))
    m_sc[...] = jnp.full(m_sc.shape, NEG, F32)
    l_sc[...] = jnp.zeros(l_sc.shape, F32)
    acc_sc[...] = jnp.zeros(acc_sc.shape, F32)

    def attend(kc, c):
        k0 = pl.multiple_of(kc * tq, tq)
        var = jnp.where(kc == qi, 0, jnp.where(kc == qi - 1, 1, 2))
        sel = s_sc[:, pl.ds(k0, tq)] <= thr_sc[...]
        for p in range(N_PAIRS):
            kp = kab_ref[pl.ds(k0, tq), p * LANES:(p + 1) * LANES]
            vp = vab_ref[pl.ds(k0, tq), p * LANES:(p + 1) * LANES]
            lg2 = _dot_t(qm_sc[p], kp)
            for hh in range(2):
                h = 2 * p + hh
                lg = jnp.where(sel, lg2[hh * tq:(hh + 1) * tq] + tiles_ref[var, h], NEG)
                _softmax_step(lg, vp, m_sc.at[h], l_sc.at[h], acc_sc.at[h])
        return c

    lax.fori_loop(0, n_ch, attend, 0)

    for p in range(N_PAIRS):
        o_lo = acc_sc[2 * p] / l_sc[2 * p]
        o_hi = acc_sc[2 * p + 1] / l_sc[2 * p + 1]
        oa_ref[:, p * LANES:(p + 1) * LANES] = jnp.where(lane < HEAD_DIM, o_lo, o_hi).astype(BF16)


def _dsa_prompt(pr, tiles, bsz, t):
    tq = ATT_TILE
    assert t % tq == 0
    topk = min(TOPK_MAX, t // 4)
    idx_bits = max(1, (t - 1).bit_length())
    r3 = lambda a: a.reshape(bsz, t, a.shape[-1])
    qtile = lambda c: pl.BlockSpec((None, tq, c), lambda b, i: (b, i, 0))
    full = lambda c: pl.BlockSpec((None, t, c), lambda b, i: (b, 0, 0))
    out = pl.pallas_call(
        functools.partial(_dsa_prompt_kernel, topk=topk, idx_bits=idx_bits),
        grid=(bsz, t // tq),
        in_specs=[qtile(IDX_HEADS * IDX_DIM), qtile(LANES), full(IDX_DIM), qtile(D_A), full(D_A),
                  full(D_A), pl.BlockSpec((3, N_HEADS_A, tq, tq), lambda b, i: (0, 0, 0, 0))],
        out_specs=qtile(D_A),
        out_shape=jax.ShapeDtypeStruct((bsz, t, D_A), BF16),
        scratch_shapes=[pltpu.VMEM((tq, t), I32), pltpu.VMEM((tq, tq), I32),
                        pltpu.VMEM((N_PAIRS, 2 * tq, LANES), BF16),
                        pltpu.VMEM((N_HEADS_A, tq, 1), F32), pltpu.VMEM((N_HEADS_A, tq, 1), F32),
                        pltpu.VMEM((N_HEADS_A, tq, LANES), F32)],
        compiler_params=_cparams(("parallel", "arbitrary")),
        name="dsa_prompt",
    )(r3(pr["qi"]), r3(pr["kw"]), r3(pr["kib"]), r3(pr["qa"]), r3(pr["kab"]), r3(pr["vab"]), tiles)
    return out.reshape(bsz * t, D_A)


def _diff_finish(acc_sc, l_sc, p, lam, g_pair, scale):
    rows = acc_sc.shape[2]
    lane = lax.broadcasted_iota(I32, (rows, LANES), 1)
    lo = lane < HEAD_DIM

    def head(v1, v2):
        return acc_sc[p, v1] / l_sc[p, v1] - lam * (acc_sc[p, v2] / l_sc[p, v2])

    o = jnp.where(lo, head(0, 1), head(2, 3))
    sq = o * o
    ms_lo = jnp.sum(jnp.where(lo, sq, 0.0), axis=1, keepdims=True) * (1.0 / HEAD_DIM)
    ms_hi = jnp.sum(jnp.where(lo, 0.0, sq), axis=1, keepdims=True) * (1.0 / HEAD_DIM)
    r = jnp.where(lo, lax.rsqrt(ms_lo + EPS), lax.rsqrt(ms_hi + EPS))
    return o * r * g_pair * scale


def _diff_prompt_kernel(lam_ref, qb_ref, kbb_ref, vbb_ref, tiles_ref, g_ref, ob_ref,
                        qm_sc, m_sc, l_sc, acc_sc, *, out_scale):
    tq = ATT_TILE
    qi = pl.program_id(1)
    lane = lax.broadcasted_iota(I32, (tq, LANES), 1)
    for p in range(N_PAIRS):
        qp = qb_ref[:, p * LANES:(p + 1) * LANES]
        for v in range(4):
            inside = (lane >= v * DIFF_DIM) & (lane < (v + 1) * DIFF_DIM)
            qm_sc[p, v * tq:(v + 1) * tq, :] = jnp.where(inside, qp, jnp.zeros_like(qp))
    m_sc[...] = jnp.full(m_sc.shape, NEG, F32)
    l_sc[...] = jnp.zeros(l_sc.shape, F32)
    acc_sc[...] = jnp.zeros(acc_sc.shape, F32)

    def attend(kc, c):
        k0 = pl.multiple_of(kc * tq, tq)
        var = jnp.where(kc == qi, 0, jnp.where(kc == qi - 1, 1, 2))
        for p in range(N_PAIRS):
            kp = kbb_ref[pl.ds(k0, tq), p * LANES:(p + 1) * LANES]
            vp = vbb_ref[pl.ds(k0, tq), p * LANES:(p + 1) * LANES]
            for hh in range(2):
                bias = tiles_ref[var, 2 * p + hh]
                lg2 = _dot_t(qm_sc[p, 2 * hh * tq:2 * (hh + 1) * tq, :], kp)
                for c2 in range(2):
                    v = 2 * hh + c2
                    lg = lg2[c2 * tq:(c2 + 1) * tq] + bias
                    _softmax_step(lg, vp, m_sc.at[p, v], l_sc.at[p, v], acc_sc.at[p, v])
        return c

    lax.fori_loop(0, qi + 1, attend, 0)

    lam = lam_ref[0]
    for p in range(N_PAIRS):
        ob_ref[:, p * LANES:(p + 1) * LANES] = _diff_finish(
            acc_sc, l_sc, p, lam, g_ref[...], out_scale).astype(BF16)


def _diff_prompt(pr, tiles, lam, g_pair, out_scale, bsz, t):
    tq = ATT_TILE
    r3 = lambda a: a.reshape(bsz, t, a.shape[-1])
    qtile = lambda c: pl.BlockSpec((None, tq, c), lambda b, i: (b, i, 0))
    full = lambda c: pl.BlockSpec((None, t, c), lambda b, i: (b, 0, 0))
    out = pl.pallas_call(
        functools.partial(_diff_prompt_kernel, out_scale=out_scale),
        grid=(bsz, t // tq),
        in_specs=[pl.BlockSpec(memory_space=pltpu.SMEM), qtile(D_B), full(D_B), full(D_B),
                  pl.BlockSpec((3, N_HEADS_B, tq, tq), lambda b, i: (0, 0, 0, 0)),
                  pl.BlockSpec((1, LANES), lambda b, i: (0, 0))],
        out_specs=qtile(D_B),
        out_shape=jax.ShapeDtypeStruct((bsz, t, D_B), BF16),
        scratch_shapes=[pltpu.VMEM((N_PAIRS, 4 * tq, LANES), BF16),
                        pltpu.VMEM((N_PAIRS, 4, tq, 1), F32), pltpu.VMEM((N_PAIRS, 4, tq, 1), F32),
                        pltpu.VMEM((N_PAIRS, 4, tq, LANES), F32)],
        compiler_params=_cparams(("parallel", "arbitrary")),
        name="diff_prompt",
    )(lam, r3(pr["qb"]), r3(pr["kbb"]), r3(pr["vbb"]), tiles, g_pair)
    return out.reshape(bsz * t, D_B)


def _gather_chunk(page_refs):
    return jnp.concatenate([r[...] for r in page_refs], axis=0).astype(BF16)


def _idx_sample_kernel(pt_ref, qi_ref, w_ref, kin_ref, *rest, past, topk, idx_bits):
    del pt_ref
    pages = rest[:PAGES_PER_STEP]
    mb_ref, s_sc = rest[PAGES_PER_STEP:]
    g = pl.program_id(1)
    ck = PAGES_PER_STEP * LANES
    w4 = w_ref[...]

    def scores(kch):
        sc = jnp.zeros((SUBLANES, kch.shape[0]), F32)
        for h in range(IDX_HEADS):
            s = _dot_t(qi_ref[h * SUBLANES:(h + 1) * SUBLANES, :], kch)
            sc = sc + w4[:, h:h + 1] * jnp.maximum(s, 0.0)
        return _sortable_key(sc)

    s_sc[:, pl.ds(pl.multiple_of(g * ck, ck), ck)] = scores(_gather_chunk(pages))

    @pl.when(g == pl.num_programs(1) - 1)
    def _():
        row = lax.broadcasted_iota(I32, (SUBLANES, LANES), 0)
        col = lax.broadcasted_iota(I32, (SUBLANES, LANES), 1)
        s_sc[:, past:past + LANES] = jnp.where(col <= row, scores(kin_ref[...]), INT_MIN)

        def read_group(i):
            return s_sc[:, pl.ds(pl.multiple_of(i * LANES, LANES), LANES)]

        def write_group(i, v):
            s_sc[:, pl.ds(pl.multiple_of(i * LANES, LANES), LANES)] = v

        m = _topk_threshold(read_group, write_group, past // LANES + 1, SUBLANES, topk, idx_bits)
        mb_ref[...] = jnp.where(s_sc[...] <= m, 0.0, NEG)


def _page_specs(width, layer, n):
    return [pl.BlockSpec((None, None, LANES, width),
                         functools.partial(lambda b, g, pt, j: (layer, pt[b, g * PAGES_PER_STEP + j], 0, 0), j=j))
            for j in range(n)]


def _idx_sample(page_table, qi_s, w_s, kin_s, cache_kidx, layer, past, n_new):
    bd = qi_s.shape[0]
    lp = past + LANES
    n_steps = past // (PAGES_PER_STEP * LANES)
    topk = min(TOPK_MAX, (past + n_new) // 4)
    idx_bits = (past + n_new - 1).bit_length()
    per_b = lambda r, c: pl.BlockSpec((None, r, c), lambda b, g, pt: (b, 0, 0))
    return pl.pallas_call(
        functools.partial(_idx_sample_kernel, past=past, topk=topk, idx_bits=idx_bits),
        grid_spec=pltpu.PrefetchScalarGridSpec(
            num_scalar_prefetch=1, grid=(bd, n_steps),
            in_specs=[per_b(IDX_HEADS * SUBLANES, IDX_DIM), per_b(SUBLANES, LANES), per_b(LANES, IDX_DIM)]
            + _page_specs(IDX_DIM, layer, PAGES_PER_STEP),
            out_specs=per_b(SUBLANES, lp),
            scratch_shapes=[pltpu.VMEM((SUBLANES, lp), I32)]),
        out_shape=jax.ShapeDtypeStruct((bd, SUBLANES, lp), F32),
        compiler_params=_cparams(("parallel", "arbitrary")),
        name="idx_sample",
    )(page_table, qi_s, w_s, kin_s, *([cache_kidx] * PAGES_PER_STEP))


def _attn_sample_kernel(pt_ref, *refs, past, n_var, use_mask, diff_scale):
    del pt_ref
    refs = list(refs)
    lam_ref = refs.pop(0) if diff_scale is not None else None
    qm_ref = refs.pop(0)
    mb_ref = refs.pop(0) if use_mask else None
    bias_ref, kn_ref, vn_ref = refs[:3]
    refs = refs[3:]
    g_ref = refs.pop(0) if diff_scale is not None else None
    kpages = refs[:PAGES_PER_STEP]
    vpages = refs[PAGES_PER_STEP:2 * PAGES_PER_STEP]
    o_ref, m_sc, l_sc, acc_sc = refs[2 * PAGES_PER_STEP:]
    g = pl.program_id(1)
    ck = PAGES_PER_STEP * LANES
    rows = SUBLANES

    @pl.when(g == 0)
    def _():
        m_sc[...] = jnp.full(m_sc.shape, NEG, F32)
        l_sc[...] = jnp.zeros(l_sc.shape, F32)
        acc_sc[...] = jnp.zeros(acc_sc.shape, F32)

    def attend(kch, vch, c0, width):
        mb = mb_ref[:, pl.ds(c0, width)] if use_mask else None
        for p in range(N_PAIRS):
            kp = kch[:, p * LANES:(p + 1) * LANES]
            vp = vch[:, p * LANES:(p + 1) * LANES]
            lg_all = _dot_t(qm_ref[p], kp)
            for v in range(n_var):
                h = 2 * p + v // (n_var // 2)
                lg = lg_all[v * rows:(v + 1) * rows] + bias_ref[h, :, pl.ds(c0, width)]
                if use_mask:
                    lg = lg + mb
                _softmax_step(lg, vp, m_sc.at[p, v], l_sc.at[p, v], acc_sc.at[p, v])

    attend(_gather_chunk(kpages), _gather_chunk(vpages), pl.multiple_of(g * ck, ck), ck)

    @pl.when(g == pl.num_programs(1) - 1)
    def _():
        attend(kn_ref[...], vn_ref[...], past, LANES)
        lane = lax.broadcasted_iota(I32, (rows, LANES), 1)
        for p in range(N_PAIRS):
            if diff_scale is None:
                o = jnp.where(lane < HEAD_DIM, acc_sc[p, 0] / l_sc[p, 0], acc_sc[p, 1] / l_sc[p, 1])
            else:
                o = _diff_finish(acc_sc, l_sc, p, lam_ref[0], g_ref[...], diff_scale)
            o_ref[:, p * LANES:(p + 1) * LANES] = o


def _attn_sample(page_table, qm_s, mb, bias_s, kn_s, vn_s, cache_k, cache_v, layer, past,
                 lam=None, g_pair=None, diff_scale=None):
    bd = qm_s.shape[0]
    n_var = qm_s.shape[2] // SUBLANES
    lp = past + LANES
    n_steps = past // (PAGES_PER_STEP * LANES)
    use_mask = mb is not None
    per_b = lambda *s: pl.BlockSpec((None,) + s, lambda b, g, pt: (b,) + (0,) * len(s))
    const = lambda *s: pl.BlockSpec(s, lambda b, g, pt: (0,) * len(s))
    args, specs = [], []
    if diff_scale is not None:
        args.append(lam)
        specs.append(pl.BlockSpec(memory_space=pltpu.SMEM))
    args.append(qm_s)
    specs.append(per_b(N_PAIRS, n_var * SUBLANES, LANES))
    if use_mask:
        args.append(mb)
        specs.append(per_b(SUBLANES, lp))
    args += [bias_s, kn_s, vn_s]
    specs += [const(N_HEADS_A, SUBLANES, lp), per_b(LANES, D_A), per_b(LANES, D_A)]
    if diff_scale is not None:
        args.append(g_pair)
        specs.append(const(1, LANES))
    args += [cache_k] * PAGES_PER_STEP + [cache_v] * PAGES_PER_STEP
    specs += _page_specs(D_A, layer, PAGES_PER_STEP) + _page_specs(D_A, layer, PAGES_PER_STEP)
    return pl.pallas_call(
        functools.partial(_attn_sample_kernel, past=past, n_var=n_var, use_mask=use_mask,
                          diff_scale=diff_scale),
        grid_spec=pltpu.PrefetchScalarGridSpec(
            num_scalar_prefetch=1, grid=(bd, n_steps), in_specs=specs,
            out_specs=per_b(SUBLANES, D_A),
            scratch_shapes=[pltpu.VMEM((N_PAIRS, n_var, SUBLANES, 1), F32),
                            pltpu.VMEM((N_PAIRS, n_var, SUBLANES, 1), F32),
                            pltpu.VMEM((N_PAIRS, n_var, SUBLANES, LANES), F32)]),
        out_shape=jax.ShapeDtypeStruct((bd, SUBLANES, D_A), F32),
        compiler_params=_cparams(("parallel", "arbitrary")),
        name="attn_sample_diff" if diff_scale is not None else "attn_sample_dsa",
    )(page_table, *args)


def _outproj_kernel(x_ref, oa_ref, ob_ref, u_ref, a_ref, b_ref, gb_ref, cw_ref, wo_ref, o_ref,
                    *, seq, per_row_state):
    tm = x_ref.shape[0]
    u = u_ref[...]
    row = lax.broadcasted_iota(I32, (tm, D_C), 0)
    um1 = pltpu.roll(u, 1, axis=0)
    um2 = pltpu.roll(u, 2, axis=0)
    if per_row_state:
        t = row % seq
        um1 = jnp.where(t == 0, a_ref[...], um1)
        um2 = jnp.where(t < 2, b_ref[...], um2)
    else:
        first = (pl.program_id(0) % (seq // tm)) == 0
        prev = jnp.where(first, b_ref[...], a_ref[...])
        p6 = prev[SUBLANES - 2:SUBLANES - 1, :]
        p7 = prev[SUBLANES - 1:SUBLANES, :]
        um1 = jnp.where(row == 0, p7, um1)
        um2 = jnp.where(row == 0, p6, jnp.where(row == 1, p7, um2))
    y = cw_ref[0:1, :] * um2 + cw_ref[1:2, :] * um1 + cw_ref[2:3, :] * u
    oc = (gb_ref[...] * y).astype(BF16)
    mixed = (_dot(oa_ref[...], wo_ref[0:D_A, :]) + _dot(ob_ref[...], wo_ref[D_A:D_A + D_B, :])
             + _dot(oc, wo_ref[D_A + D_B:D_MODEL, :]))
    o_ref[...] = x_ref[...] + mixed


def _outproj(x, oa, ob, u, gb, cw, wo, a, b, seq, per_row_state):
    n = x.shape[0]
    tm = min(512, n)
    assert n % tm == 0
    rowt = lambda c: pl.BlockSpec((tm, c), lambda i: (i, 0))
    if per_row_state:
        a_spec, b_spec = rowt(D_C), rowt(D_C)
    else:
        assert seq % tm == 0
        a_spec = pl.BlockSpec((SUBLANES, D_C), lambda i: (jnp.maximum(i * (tm // SUBLANES) - 1, 0), 0))
        b_spec = pl.BlockSpec((None, SUBLANES, D_C), lambda i: (i // (seq // tm), 0, 0))
    return pl.pallas_call(
        functools.partial(_outproj_kernel, seq=seq, per_row_state=per_row_state),
        grid=(n // tm,),
        in_specs=[rowt(D_MODEL), rowt(D_A), rowt(D_B), rowt(D_C), a_spec, b_spec, rowt(D_C),
                  pl.BlockSpec((SUBLANES, D_C), lambda i: (0, 0)),
                  pl.BlockSpec((D_MODEL, D_MODEL), lambda i: (0, 0))],
        out_specs=rowt(D_MODEL),
        out_shape=jax.ShapeDtypeStruct((n, D_MODEL), F32),
        compiler_params=_cparams(("parallel",)),
        name="outproj",
    )(x, oa, ob, u, a, b, gb, cw, wo)


def _silu(a):
    return a / (1.0 + jnp.exp(-a))


def _ffn_kernel(x_ref, g_ref, w1_ref, w3_ref, w2_ref, o_ref, *, n_chunks):
    x = x_ref[...]
    h = _rms(x, g_ref[...]).astype(BF16)
    tf = w1_ref.shape[1] // n_chunks
    out = x
    for c in range(n_chunks):
        a1 = _dot(h, w1_ref[:, c * tf:(c + 1) * tf])
        a3 = _dot(h, w3_ref[:, c * tf:(c + 1) * tf])
        out = out + _dot((_silu(a1) * a3).astype(BF16), w2_ref[c * tf:(c + 1) * tf, :])
    o_ref[...] = out


def _ffn(x, g, w1, w3, w2):
    n = x.shape[0]
    f = w1.shape[1]
    tm = min(256, n)
    n_chunks = 2 if f % (2 * LANES) == 0 else 1
    const = lambda s: pl.BlockSpec(s, lambda i: (0, 0), pipeline_mode=pl.Buffered(1))
    return pl.pallas_call(
        functools.partial(_ffn_kernel, n_chunks=n_chunks),
        grid=(n // tm,),
        in_specs=[pl.BlockSpec((tm, D_MODEL), lambda i: (i, 0)), pl.BlockSpec((1, D_MODEL), lambda i: (0, 0)),
                  const((D_MODEL, f)), const((D_MODEL, f)), const((f, D_MODEL))],
        out_specs=pl.BlockSpec((tm, D_MODEL), lambda i: (i, 0)),
        out_shape=jax.ShapeDtypeStruct((n, D_MODEL), F32),
        compiler_params=_cparams(("parallel",)),
        name="ffn",
    )(x, g, w1, w3, w2)


def _moe_kernel(x_ref, g_ref, wr_ref, br_ref, w1_ref, w3_ref, w2_ref, gf_ref, o_ref,
                h_sc, gate_sc, acc_sc):
    e = pl.program_id(1)
    f = pl.program_id(2)
    tm = x_ref.shape[0]
    lane = lax.broadcasted_iota(I32, (tm, LANES), 1)

    @pl.when((e == 0) & (f == 0))
    def _():
        h = _rms(x_ref[...], g_ref[...])
        h_sc[...] = h.astype(BF16)
        logits = jnp.dot(h, wr_ref[...], preferred_element_type=F32,
                         precision=lax.Precision.HIGHEST) + br_ref[...]
        lane_f = lane.astype(F32)
        top1 = jnp.max(logits, axis=1, keepdims=True)
        i1 = jnp.min(jnp.where(logits == top1, lane_f, float(LANES)), axis=1, keepdims=True)
        rest = jnp.where(lane_f == i1, NEG, logits)
        top2 = jnp.max(rest, axis=1, keepdims=True)
        i2 = jnp.min(jnp.where(rest == top2, lane_f, float(LANES)), axis=1, keepdims=True)
        e2 = jnp.exp(top2 - top1)
        den = 1.0 + e2
        gate_sc[...] = jnp.where(lane_f == i1, 1.0 / den, 0.0) + jnp.where(lane_f == i2, e2 / den, 0.0)
        acc_sc[...] = jnp.zeros(acc_sc.shape, F32)

    h = h_sc[...]
    act = (_silu(_dot(h, w1_ref[...])) * _dot(h, w3_ref[...])).astype(BF16)
    ge = jnp.sum(jnp.where(lane == e, gate_sc[...], 0.0), axis=1, keepdims=True)
    acc_sc[...] += ge * _dot(act, w2_ref[...])

    @pl.when((e == pl.num_programs(1) - 1) & (f == pl.num_programs(2) - 1))
    def _():
        o_ref[...] = _rms(x_ref[...] + acc_sc[...], gf_ref[...])


def _moe_final(x, g, wr, br, w1, w3, w2, g_final):
    n = x.shape[0]
    n_exp, _, f = w1.shape
    tm = min(512, n)
    n_chunks = 2 if f % (2 * LANES) == 0 else 1
    tf = f // n_chunks
    tok = lambda c: pl.BlockSpec((tm, c), lambda i, e, j: (i, 0))
    const = lambda r, c: pl.BlockSpec((r, c), lambda i, e, j: (0, 0))
    return pl.pallas_call(
        _moe_kernel,
        grid=(n // tm, n_exp, n_chunks),
        in_specs=[tok(D_MODEL), const(1, D_MODEL), const(D_MODEL, LANES), const(1, LANES),
                  pl.BlockSpec((None, D_MODEL, tf), lambda i, e, j: (e, 0, j)),
                  pl.BlockSpec((None, D_MODEL, tf), lambda i, e, j: (e, 0, j)),
                  pl.BlockSpec((None, tf, D_MODEL), lambda i, e, j: (e, j, 0)),
                  const(1, D_MODEL)],
        out_specs=tok(D_MODEL),
        out_shape=jax.ShapeDtypeStruct((n, D_MODEL), F32),
        scratch_shapes=[pltpu.VMEM((tm, D_MODEL), BF16), pltpu.VMEM((tm, LANES), F32),
                        pltpu.VMEM((tm, D_MODEL), F32)],
        compiler_params=_cparams(("parallel", "arbitrary", "arbitrary")),
        name="moe_final",
    )(x, g, wr, br, w1, w3, w2, g_final)


def _t5_bucket(n):
    max_exact = NUM_BUCKETS // 2
    nf = jnp.maximum(n, max_exact).astype(F32)
    large = max_exact + (jnp.log(nf / max_exact) / math.log(MAX_DISTANCE / max_exact)
                         * (NUM_BUCKETS - max_exact)).astype(I32)
    return jnp.where(n < max_exact, n, jnp.minimum(large, NUM_BUCKETS - 1))


def _bias_of_distance(rel_bias, d):
    b = jnp.moveaxis(rel_bias[_t5_bucket(jnp.maximum(d, 0))], -1, 0) * LOG2E
    return jnp.where(d < 0, NEG, b)


def _prompt_tiles(rel_bias):
    i = jnp.arange(ATT_TILE, dtype=I32)
    d = i[:, None] - i[None, :]
    tiles = jnp.stack([_bias_of_distance(rel_bias, d + v * ATT_TILE) for v in range(3)])
    return tiles[:, :N_HEADS_A], tiles[:, N_HEADS_A:]


def _sample_bias(rel_bias, past, n_new):
    i = jnp.arange(SUBLANES, dtype=I32)[:, None]
    k = jnp.arange(past + LANES, dtype=I32)[None, :]
    b = _bias_of_distance(rel_bias, past + i - k)
    b = jnp.where(k < past + n_new, b, NEG)
    return b[:N_HEADS_A], b[N_HEADS_A:]


def _pad_rows(a, rows):
    pad = [(0, 0)] * a.ndim
    pad[-2] = (0, rows - a.shape[-2])
    return jnp.pad(a, pad)


def _masked_queries(q, n_var):
    bd, t, _ = q.shape
    qp = _pad_rows(q.reshape(bd, t, N_PAIRS, LANES).transpose(0, 2, 1, 3), SUBLANES)
    lane = jnp.arange(LANES)
    w = LANES // n_var
    parts = [jnp.where((lane >= v * w) & (lane < (v + 1) * w), qp, jnp.zeros_like(qp)) for v in range(n_var)]
    return jnp.concatenate(parts, axis=2)


def _layer_weights(l, w_in, w_out, conv_w, subln, lam_q1, lam_k1, lam_q2, lam_k2):
    p_orig = w_in.shape[-1]
    split = C_KW + IDX_DIM + IDX_HEADS
    w = jnp.concatenate([w_in[l][:, :split], jnp.zeros((D_MODEL, C_END - p_orig), F32),
                         w_in[l][:, split:]], axis=1).astype(BF16)
    lam_init = 0.8 - 0.6 * math.exp(-0.3 * l)
    lam = (jnp.exp(jnp.sum(lam_q1[l] * lam_k1[l])) - jnp.exp(jnp.sum(lam_q2[l] * lam_k2[l])) + lam_init)
    return dict(w_in=w, w_out=w_out[l].astype(BF16), cw=_pad_rows(conv_w[l], SUBLANES),
                g_pair=jnp.tile(subln[l], 2).reshape(1, LANES), lam=lam.reshape(1).astype(F32),
                out_scale=1.0 - lam_init)


def kernel(x_prompt, x_sample, cache_a_k, cache_a_v, cache_a_kidx, cache_b_k, cache_b_v, state_conv,
           page_table, w_in, w_out, norm_mix, norm_ffn, norm_final, rel_bias, lam_q1, lam_k1, lam_q2,
           lam_k2, subln, conv_w, ffn_w1, ffn_w3, ffn_w2, moe_router, moe_router_b, moe_w1, moe_w3,
           moe_w2):
    depth = w_in.shape[0]
    assert depth == 2, "layer 0 is dense, layer 1 is MoE and is followed by the final norm"
    bsz, seq, _ = x_prompt.shape
    bd, n_new, _ = x_sample.shape
    n_pool = cache_a_k.shape[1]
    past = page_table.shape[1] * LANES
    assert cache_a_k.shape[2] == LANES and n_new <= SUBLANES

    lw = [_layer_weights(l, w_in, w_out, conv_w, subln, lam_q1, lam_k1, lam_q2, lam_k2) for l in range(depth)]
    ffn = (ffn_w1[0].astype(BF16), ffn_w3[0].astype(BF16), ffn_w2[0].astype(BF16))
    moe = (jnp.pad(moe_router[0], ((0, 0), (0, LANES - N_EXPERTS))),
           jnp.pad(moe_router_b[0], (0, LANES - N_EXPERTS), constant_values=NEG).reshape(1, LANES),
           moe_w1[0].astype(BF16), moe_w3[0].astype(BF16), moe_w2[0].astype(BF16))
    row = lambda v: v.reshape(1, -1)
    tiles_a, tiles_b = _prompt_tiles(rel_bias)
    sbias_a, sbias_b = _sample_bias(rel_bias, past, n_new)
    c4 = lambda c, w: c.reshape(depth, n_pool, LANES, w)
    ca_k, ca_v, cb_k, cb_v = (c4(c, D_A) for c in (cache_a_k, cache_a_v, cache_b_k, cache_b_v))

    def ffn_block(l, x):
        if l == 0:
            return _ffn(x, row(norm_ffn[l]), *ffn)
        return _moe_final(x, row(norm_ffn[l]), *moe, row(norm_final))

    def rows_of(pr, b, t):
        r = lambda a, *s: a.reshape(b, t, *s)
        return (r(pr["ka"], N_HEADS_A, HEAD_DIM), r(pr["va"], N_HEADS_A, HEAD_DIM),
                r(pr["kw"], LANES)[..., :IDX_DIM], r(pr["kb"], N_HEADS_B, HEAD_DIM),
                r(pr["vb"], N_HEADS_B, HEAD_DIM), r(pr["u"], D_C)[:, t - (CONV_WIDTH - 1):])

    x = x_prompt.reshape(bsz * seq, D_MODEL)
    rows_p = []
    zeros_state = jnp.zeros((bsz, SUBLANES, D_C), F32)
    for l in range(depth):
        pr = _proj(x, row(norm_mix[l]), lw[l]["w_in"])
        rows_p.append(rows_of(pr, bsz, seq))
        oa = _dsa_prompt(pr, tiles_a, bsz, seq)
        ob = _diff_prompt(pr, tiles_b, lw[l]["lam"], lw[l]["g_pair"], lw[l]["out_scale"], bsz, seq)
        x = _outproj(x, oa, ob, pr["u"], pr["gb"], lw[l]["cw"], lw[l]["w_out"], pr["u"], zeros_state,
                     seq, False)
        x = ffn_block(l, x)
    y_prompt = x.reshape(bsz, seq, D_MODEL)

    x = x_sample.reshape(bd * n_new, D_MODEL)
    rows_s = []
    for l in range(depth):
        pr = _proj(x, row(norm_mix[l]), lw[l]["w_in"])
        rows_s.append(rows_of(pr, bd, n_new))
        r3 = lambda a: a.reshape(bd, n_new, a.shape[-1])
        new_rows = lambda a: _pad_rows(r3(a), LANES)
        qi_s = _pad_rows(r3(pr["qi"]).reshape(bd, n_new, IDX_HEADS, IDX_DIM).transpose(0, 2, 1, 3),
                         SUBLANES).reshape(bd, IDX_HEADS * SUBLANES, IDX_DIM)
        w_s = jnp.pad(_pad_rows(r3(pr["kw"])[..., IDX_DIM:IDX_DIM + IDX_HEADS], SUBLANES)
                      * IDX_HEADS ** -0.5, ((0, 0), (0, 0), (0, LANES - IDX_HEADS)))
        mb = _idx_sample(page_table, qi_s, w_s, new_rows(pr["kib"]), cache_a_kidx, l, past, n_new)
        oa = _attn_sample(page_table, _masked_queries(r3(pr["qa"]), 2), mb, sbias_a,
                          new_rows(pr["kab"]), new_rows(pr["vab"]), ca_k, ca_v, l, past)
        ob = _attn_sample(page_table, _masked_queries(r3(pr["qb"]), 4), None, sbias_b,
                          new_rows(pr["kbb"]), new_rows(pr["vbb"]), cb_k, cb_v, l, past,
                          lam=lw[l]["lam"], g_pair=lw[l]["g_pair"], diff_scale=lw[l]["out_scale"])
        flat = lambda o: o[:, :n_new].reshape(bd * n_new, o.shape[-1]).astype(BF16)
        st = state_conv[l]
        zero = jnp.zeros((bd, n_new - 1, D_C), F32)
        a = jnp.concatenate([st[:, 1:2], zero], axis=1).reshape(bd * n_new, D_C)
        b = jnp.concatenate([st, zero[:, 1:]], axis=1).reshape(bd * n_new, D_C)
        x = _outproj(x, flat(oa), flat(ob), pr["u"], pr["gb"], lw[l]["cw"], lw[l]["w_out"], a, b,
                     n_new, True)
        x = ffn_block(l, x)
    y_sample = x.reshape(bd, n_new, D_MODEL)

    stack = lambda rows: tuple(jnp.stack([r[i] for r in rows]) for i in range(6))
    return (y_prompt, y_sample) + stack(rows_p) + stack(rows_s)
```

```python
import functools
import math

import jax
import jax.numpy as jnp
from jax import lax
from jax.experimental import pallas as pl
from jax.experimental.pallas import tpu as pltpu

F32 = jnp.float32
BF16 = jnp.bfloat16
I32 = jnp.int32

D_MODEL = 1024
HEAD_DIM = 64
N_HEADS_A = 6
N_HEADS_B = 6
D_A = N_HEADS_A * HEAD_DIM
D_B = N_HEADS_B * HEAD_DIM
D_C = D_MODEL - D_A - D_B
DIFF_DIM = HEAD_DIM // 2
IDX_HEADS = 4
IDX_DIM = 64
TOPK_MAX = 256
CONV_WIDTH = 3
NUM_BUCKETS = 32
MAX_DISTANCE = 128
N_EXPERTS = 8
EPS = 1e-6

LANES = 128
SUBLANES = 8
VMEM_LIMIT = 56 * 1024 * 1024
LOG2E = 1.4426950408889634
NEG = -1e30
INT_MIN = -2**31
BIG_IDX = 2**30

ATT_TILE = 256
PAGES_PER_STEP = 8
N_PAIRS = D_A // LANES

R_QA, R_QI, R_QB, R_KA, R_KB, R_KI, R_HC, R_GB, R_GC, R_END = (
    0, 384, 640, 1024, 1408, 1792, 1920, 2176, 2432, 2688)
T_KA, T_VA, T_KB, T_VB, T_KW, T_END = 0, 384, 768, 1152, 1536, 1664


def _cparams(sem):
    return pltpu.CompilerParams(dimension_semantics=sem, vmem_limit_bytes=VMEM_LIMIT)


def _dot_t(a, b):
    return lax.dot_general(a, b, (((1,), (1,)), ((), ())), preferred_element_type=F32)


def _dot(a, b):
    return jnp.dot(a, b, preferred_element_type=F32)


def _rms(x, g):
    ms = jnp.mean(x * x, axis=-1, keepdims=True)
    return x * lax.rsqrt(ms + EPS) * g


def _proj_kernel(x_ref, g_ref, w_ref, wt_ref, qa_ref, qi_ref, qb_ref, kab_ref, kbb_ref, kib_ref,
                 u_ref, gb_ref, kat_ref, vat_ref, vatb_ref, kbt_ref, vbt_ref, vbtb_ref, kwt_ref):
    h = _rms(x_ref[...], g_ref[...]).astype(BF16)

    def mm(a, b):
        return _dot(h, w_ref[:, a:b])

    def mt(a, b):
        return _dot_t(wt_ref[a:b, :], h)

    qa_ref[...] = (mm(R_QA, R_QI) * (HEAD_DIM ** -0.5 * LOG2E)).astype(BF16)
    qi_ref[...] = (mm(R_QI, R_QB) * IDX_DIM ** -0.5).astype(BF16)
    qb_ref[...] = (mm(R_QB, R_KA) * (DIFF_DIM ** -0.5 * LOG2E)).astype(BF16)
    kab_ref[...] = mm(R_KA, R_KB).astype(BF16)
    kbb_ref[...] = mm(R_KB, R_KI).astype(BF16)
    kib_ref[...] = mm(R_KI, R_HC)[:, :IDX_DIM].astype(BF16)
    hc = mm(R_HC, R_GB)
    gb_ref[...] = mm(R_GB, R_GC)
    u_ref[...] = mm(R_GC, R_END) * hc
    kat_ref[...] = mt(T_KA, T_VA)
    va = mt(T_VA, T_KB)
    vat_ref[...] = va
    vatb_ref[...] = va.astype(BF16)
    kbt_ref[...] = mt(T_KB, T_VB)
    vb = mt(T_VB, T_KW)
    vbt_ref[...] = vb
    vbtb_ref[...] = vb.astype(BF16)
    kwt_ref[...] = mt(T_KW, T_END)


def _proj(x, g, w, wt, bsz, t):
    tm = min(ATT_TILE, t)
    assert t % tm == 0
    tpb = t // tm
    tok = dict(qa=(D_A, BF16), qi=(IDX_HEADS * IDX_DIM, BF16), qb=(D_B, BF16), kab=(D_A, BF16),
               kbb=(D_B, BF16), kib=(IDX_DIM, BF16), u=(D_C, F32), gb=(D_C, F32))
    feat = dict(kat=(D_A, F32), vat=(D_A, F32), vatb=(D_A, BF16), kbt=(D_B, F32), vbt=(D_B, F32),
                vbtb=(D_B, BF16), kwt=(LANES, F32))
    outs = pl.pallas_call(
        _proj_kernel,
        grid=(bsz * tpb,),
        in_specs=[pl.BlockSpec((tm, D_MODEL), lambda i: (i, 0)),
                  pl.BlockSpec((1, D_MODEL), lambda i: (0, 0)),
                  pl.BlockSpec((D_MODEL, R_END), lambda i: (0, 0)),
                  pl.BlockSpec((T_END, D_MODEL), lambda i: (0, 0))],
        out_specs=[pl.BlockSpec((tm, c), lambda i: (i, 0)) for c, _ in tok.values()]
        + [pl.BlockSpec((None, c, tm), lambda i: (i // tpb, 0, i % tpb)) for c, _ in feat.values()],
        out_shape=[jax.ShapeDtypeStruct((bsz * t, c), d) for c, d in tok.values()]
        + [jax.ShapeDtypeStruct((bsz, c, t), d) for c, d in feat.values()],
        compiler_params=_cparams(("parallel",)),
        name="proj",
    )(x, g, w, wt)
    return dict(zip(list(tok) + list(feat), outs))


def _sortable_key(score):
    score = jnp.where(score == 0.0, 0.0, score)
    bits = pltpu.bitcast(score, I32)
    return bits ^ ((bits >> 31) & 0x7FFFFFFF)


def _topk_search(count, transform, shape, topk, idx_bits):
    kf = float(topk)
    t = jnp.where(count(lambda blk, c: blk >= 0, None) >= kf, 0, INT_MIN).astype(I32)

    def value_bit(i, t):
        cand = t | lax.shift_left(jnp.int32(1), 30 - i)
        return jnp.where(count(lambda blk, c: blk >= c, cand) >= kf, cand, t)

    t = lax.fori_loop(0, 31, value_bit, t)
    transform(jnp.maximum(t, INT_MIN + 1))

    def index_bit(i, m):
        cand = m | lax.shift_left(jnp.int32(1), idx_bits - 1 - i)
        return jnp.where(count(lambda blk, c: blk < c, cand) < kf, cand, m)

    return lax.fori_loop(0, idx_bits, index_bit, jnp.zeros(shape, I32))


def _rewrite_keys(blk, t, idx):
    return jnp.where(blk > t, -1, jnp.where(blk == t, idx, BIG_IDX))


def _softmax_step_t(lg_of, vt, m_ref, l_ref, acc_ref):
    for j in range(m_ref.shape[-1] // LANES):
        cols = slice(j * LANES, (j + 1) * LANES)
        lg = lg_of(cols)
        m_old = m_ref[:, cols]
        m_new = jnp.maximum(m_old, jnp.max(lg, axis=0, keepdims=True))
        alpha = jnp.exp2(m_old - m_new)
        p = jnp.exp2(lg - m_new)
        l_ref[:, cols] = alpha * l_ref[:, cols] + jnp.sum(p, axis=0, keepdims=True)
        acc_ref[:, cols] = alpha * acc_ref[:, cols] + _dot(vt, p.astype(BF16))
        m_ref[:, cols] = m_new


def _dsa_prompt_kernel(qi_ref, kwt_ref, kib_ref, qa_ref, kab_ref, vat_ref, tiles_ref, oa_ref,
                       s_sc, mb_sc, lg_sc, qm_sc, m_sc, l_sc, acc_sc, *, topk, idx_bits):
    tq = ATT_TILE
    qi = pl.program_id(1)
    n_ch = qi + 1
    q0 = qi * tq
    row = lax.broadcasted_iota(I32, (tq, tq), 0)
    col = lax.broadcasted_iota(I32, (tq, tq), 1)

    w4 = kwt_ref[IDX_DIM:IDX_DIM + SUBLANES, :] * IDX_HEADS ** -0.5

    def score_chunk(kc, c):
        k0 = pl.multiple_of(kc * tq, tq)
        kch = kib_ref[pl.ds(k0, tq), :]
        sc = jnp.zeros((tq, tq), F32)
        for h in range(IDX_HEADS):
            s = _dot_t(kch, qi_ref[:, h * IDX_DIM:(h + 1) * IDX_DIM])
            sc = sc + w4[h:h + 1, :] * jnp.maximum(s, 0.0)
        valid = (k0 + row) <= (q0 + col)
        s_sc[pl.ds(k0, tq), :] = jnp.where(valid, _sortable_key(sc), INT_MIN)
        return c

    lax.fori_loop(0, n_ch, score_chunk, 0)

    def count(pred, cand):
        def body(kc, acc):
            k0 = pl.multiple_of(kc * tq, tq)
            ones = jnp.where(pred(s_sc[pl.ds(k0, tq), :], cand), 1, 0)
            parts = [ones[g * SUBLANES:(g + 1) * SUBLANES] for g in range(tq // SUBLANES)]
            while len(parts) > 1:
                parts = [a + b for a, b in zip(parts[::2], parts[1::2])]
            return acc + parts[0]

        acc = lax.fori_loop(0, n_ch, body, jnp.zeros((SUBLANES, tq), I32))
        return jnp.sum(acc.astype(F32), axis=0, keepdims=True)

    def transform(t):
        def body(kc, c):
            k0 = pl.multiple_of(kc * tq, tq)
            s_sc[pl.ds(k0, tq), :] = _rewrite_keys(s_sc[pl.ds(k0, tq), :], t, k0 + row)
            return c
        lax.fori_loop(0, n_ch, body, 0)

    thr = _topk_search(count, transform, (1, tq), topk, idx_bits)

    lane = lax.broadcasted_iota(I32, (tq, LANES), 1)
    for p in range(N_PAIRS):
        qp = qa_ref[:, p * LANES:(p + 1) * LANES]
        qm_sc[p, 0:tq, :] = jnp.where(lane < HEAD_DIM, qp, jnp.zeros_like(qp))
        qm_sc[p, tq:2 * tq, :] = jnp.where(lane >= HEAD_DIM, qp, jnp.zeros_like(qp))
    m_sc[...] = jnp.full(m_sc.shape, NEG, F32)
    l_sc[...] = jnp.zeros(l_sc.shape, F32)
    acc_sc[...] = jnp.zeros(acc_sc.shape, F32)

    def logits_into(slot, kc):
        k0 = pl.multiple_of(kc * tq, tq)
        for p in range(N_PAIRS):
            kp = kab_ref[pl.ds(k0, tq), p * LANES:(p + 1) * LANES]
            lg_sc[slot, :, 2 * p * tq:2 * (p + 1) * tq] = _dot_t(kp, qm_sc[p])

    logits_into(0, 0)

    def attend(kc, c):
        k0 = pl.multiple_of(kc * tq, tq)
        slot = kc % 2
        var = jnp.where(kc == qi, 0, jnp.where(kc == qi - 1, 1, 2))
        mb_sc[...] = jnp.where(s_sc[pl.ds(k0, tq), :] <= thr, 0.0, NEG)
        for h in range(N_HEADS_A):
            p = h // 2
            vt = vat_ref[p * LANES:(p + 1) * LANES, pl.ds(k0, tq)]

            def lg_of(cols, h=h):
                shifted = slice(h * tq + cols.start, h * tq + cols.stop)
                return lg_sc[slot, :, shifted] + tiles_ref[var, h, :, cols] + mb_sc[:, cols]

            _softmax_step_t(lg_of, vt, m_sc.at[h], l_sc.at[h], acc_sc.at[h])
        logits_into(1 - slot, jnp.minimum(kc + 1, n_ch - 1))
        return c

    lax.fori_loop(0, n_ch, attend, 0)

    srow = lax.broadcasted_iota(I32, (LANES, tq), 0)
    for p in range(N_PAIRS):
        o_lo = acc_sc[2 * p] / l_sc[2 * p]
        o_hi = acc_sc[2 * p + 1] / l_sc[2 * p + 1]
        o_t = jnp.where(srow < HEAD_DIM, o_lo, o_hi)
        oa_ref[:, p * LANES:(p + 1) * LANES] = o_t.T.astype(BF16)


def _dsa_prompt(pr, tiles, bsz, t):
    tq = ATT_TILE
    assert t % tq == 0
    topk = min(TOPK_MAX, t // 4)
    idx_bits = max(1, (t - 1).bit_length())
    r3 = lambda a: a.reshape(bsz, t, a.shape[-1])
    qtile = lambda c: pl.BlockSpec((None, tq, c), lambda b, i: (b, i, 0))
    full = lambda c: pl.BlockSpec((None, t, c), lambda b, i: (b, 0, 0))
    out = pl.pallas_call(
        functools.partial(_dsa_prompt_kernel, topk=topk, idx_bits=idx_bits),
        grid=(bsz, t // tq),
        in_specs=[qtile(IDX_HEADS * IDX_DIM),
                  pl.BlockSpec((None, LANES, tq), lambda b, i: (b, 0, i)),
                  full(IDX_DIM), qtile(D_A), full(D_A),
                  pl.BlockSpec((None, D_A, t), lambda b, i: (b, 0, 0)),
                  pl.BlockSpec((3, N_HEADS_A, tq, tq), lambda b, i: (0, 0, 0, 0))],
        out_specs=qtile(D_A),
        out_shape=jax.ShapeDtypeStruct((bsz, t, D_A), BF16),
        scratch_shapes=[pltpu.VMEM((t, tq), I32), pltpu.VMEM((tq, tq), F32),
                        pltpu.VMEM((2, tq, N_HEADS_A * tq), F32),
                        pltpu.VMEM((N_PAIRS, 2 * tq, LANES), BF16),
                        pltpu.VMEM((N_HEADS_A, 1, tq), F32), pltpu.VMEM((N_HEADS_A, 1, tq), F32),
                        pltpu.VMEM((N_HEADS_A, LANES, tq), F32)],
        compiler_params=_cparams(("parallel", "arbitrary")),
        name="dsa_prompt",
    )(r3(pr["qi"]), pr["kwt"], r3(pr["kib"]), r3(pr["qa"]), r3(pr["kab"]), pr["vatb"], tiles)
    return out.reshape(bsz * t, D_A)


def _diff_prompt_kernel(lam_ref, qb_ref, kbb_ref, vbt_ref, tiles_ref, g_ref, ob_ref,
                        lg_sc, qm_sc, m_sc, l_sc, acc_sc, *, out_scale):
    tq = ATT_TILE
    qi = pl.program_id(1)
    lane = lax.broadcasted_iota(I32, (tq, LANES), 1)
    for p in range(N_PAIRS):
        qp = qb_ref[:, p * LANES:(p + 1) * LANES]
        for v in range(4):
            inside = (lane >= v * DIFF_DIM) & (lane < (v + 1) * DIFF_DIM)
            qm_sc[p, v * tq:(v + 1) * tq, :] = jnp.where(inside, qp, jnp.zeros_like(qp))
    m_sc[...] = jnp.full(m_sc.shape, NEG, F32)
    l_sc[...] = jnp.zeros(l_sc.shape, F32)
    acc_sc[...] = jnp.zeros(acc_sc.shape, F32)

    n_ch = qi + 1

    def logits_into(slot, kc):
        k0 = pl.multiple_of(kc * tq, tq)
        for p in range(N_PAIRS):
            kp = kbb_ref[pl.ds(k0, tq), p * LANES:(p + 1) * LANES]
            for hh in range(2):
                u0 = (4 * p + 2 * hh) * tq
                lg_sc[slot, :, u0:u0 + 2 * tq] = _dot_t(kp, qm_sc[p, 2 * hh * tq:2 * (hh + 1) * tq, :])

    logits_into(0, 0)

    def attend(kc, c):
        k0 = pl.multiple_of(kc * tq, tq)
        slot = kc % 2
        var = jnp.where(kc == qi, 0, jnp.where(kc == qi - 1, 1, 2))
        for p in range(N_PAIRS):
            vt = vbt_ref[p * LANES:(p + 1) * LANES, pl.ds(k0, tq)]
            for v in range(4):
                def lg_of(cols, u=4 * p + v, h=2 * p + v // 2):
                    shifted = slice(u * tq + cols.start, u * tq + cols.stop)
                    return lg_sc[slot, :, shifted] + tiles_ref[var, h, :, cols]

                _softmax_step_t(lg_of, vt, m_sc.at[p, v], l_sc.at[p, v], acc_sc.at[p, v])
        logits_into(1 - slot, jnp.minimum(kc + 1, n_ch - 1))
        return c

    lax.fori_loop(0, n_ch, attend, 0)

    lam = lam_ref[0]
    lo = lax.broadcasted_iota(I32, (LANES, tq), 0) < HEAD_DIM
    for p in range(N_PAIRS):
        def head(v1, v2):
            return acc_sc[p, v1] / l_sc[p, v1] - lam * (acc_sc[p, v2] / l_sc[p, v2])

        o = jnp.where(lo, head(0, 1), head(2, 3))
        sq = o * o
        ms_lo = jnp.sum(jnp.where(lo, sq, 0.0), axis=0, keepdims=True) * (1.0 / HEAD_DIM)
        ms_hi = jnp.sum(jnp.where(lo, 0.0, sq), axis=0, keepdims=True) * (1.0 / HEAD_DIM)
        r = jnp.where(lo, lax.rsqrt(ms_lo + EPS), lax.rsqrt(ms_hi + EPS))
        y = o * r * g_ref[...] * out_scale
        ob_ref[:, p * LANES:(p + 1) * LANES] = y.T.astype(BF16)


def _diff_prompt(pr, tiles, lam, g_col, out_scale, bsz, t):
    tq = ATT_TILE
    r3 = lambda a: a.reshape(bsz, t, a.shape[-1])
    qtile = lambda c: pl.BlockSpec((None, tq, c), lambda b, i: (b, i, 0))
    full = lambda c: pl.BlockSpec((None, t, c), lambda b, i: (b, 0, 0))
    out = pl.pallas_call(
        functools.partial(_diff_prompt_kernel, out_scale=out_scale),
        grid=(bsz, t // tq),
        in_specs=[pl.BlockSpec(memory_space=pltpu.SMEM), qtile(D_B), full(D_B),
                  pl.BlockSpec((None, D_B, t), lambda b, i: (b, 0, 0)),
                  pl.BlockSpec((3, N_HEADS_B, tq, tq), lambda b, i: (0, 0, 0, 0)),
                  pl.BlockSpec((LANES, 1), lambda b, i: (0, 0))],
        out_specs=qtile(D_B),
        out_shape=jax.ShapeDtypeStruct((bsz, t, D_B), BF16),
        scratch_shapes=[pltpu.VMEM((2, tq, 2 * N_HEADS_B * tq), F32),
                        pltpu.VMEM((N_PAIRS, 4 * tq, LANES), BF16),
                        pltpu.VMEM((N_PAIRS, 4, 1, tq), F32), pltpu.VMEM((N_PAIRS, 4, 1, tq), F32),
                        pltpu.VMEM((N_PAIRS, 4, LANES, tq), F32)],
        compiler_params=_cparams(("parallel", "arbitrary")),
        name="diff_prompt",
    )(lam, r3(pr["qb"]), r3(pr["kbb"]), pr["vbtb"], tiles, g_col)
    return out.reshape(bsz * t, D_B)


def _gather_chunk(page_refs):
    return jnp.concatenate([r[...] for r in page_refs], axis=1).astype(BF16)


def _idx_sample_kernel(pt_ref, qi_ref, w_ref, kin_ref, *rest, past, topk, idx_bits, n_rep):
    del pt_ref
    pages = rest[:PAGES_PER_STEP]
    mb_ref, s_sc = rest[PAGES_PER_STEP:]
    g = pl.program_id(1)
    ck = PAGES_PER_STEP * LANES
    w4 = w_ref[...]

    def scores(s_all):
        sc = jnp.zeros((SUBLANES, s_all.shape[1]), F32)
        for h in range(IDX_HEADS):
            sc = sc + w4[:, h:h + 1] * jnp.maximum(s_all[h * SUBLANES:(h + 1) * SUBLANES], 0.0)
        return _sortable_key(sc)

    s_sc[:, pl.ds(pl.multiple_of(g * ck, ck), ck)] = scores(_dot(qi_ref[...], _gather_chunk(pages)))

    @pl.when(g == pl.num_programs(1) - 1)
    def _():
        row = lax.broadcasted_iota(I32, (SUBLANES, LANES), 0)
        col = lax.broadcasted_iota(I32, (SUBLANES, LANES), 1)
        new = scores(_dot_t(qi_ref[...], kin_ref[...]))
        s_sc[:, past:past + LANES] = jnp.where(col <= row, new, INT_MIN)
        n_groups = past // LANES + 1

        def group(i):
            return pl.ds(pl.multiple_of(i * LANES, LANES), LANES)

        def count(pred, cand):
            cb = None if cand is None else jnp.broadcast_to(cand, (SUBLANES, LANES))
            acc = lax.fori_loop(0, n_groups,
                                lambda i, a: a + jnp.where(pred(s_sc[:, group(i)], cb), 1, 0),
                                jnp.zeros((SUBLANES, LANES), I32))
            return jnp.sum(acc.astype(F32), axis=1, keepdims=True)

        def transform(t):
            def body(i, c):
                s_sc[:, group(i)] = _rewrite_keys(s_sc[:, group(i)], t, i * LANES + col)
                return c
            lax.fori_loop(0, n_groups, body, 0)

        m = _topk_search(count, transform, (SUBLANES, 1), topk, idx_bits)
        mb = jnp.where(s_sc[...] <= m, 0.0, NEG)
        mb_ref[...] = jnp.concatenate([mb] * n_rep, axis=0)


def _page_specs(width, layer, n):
    return [pl.BlockSpec((None, None, width, LANES),
                         functools.partial(lambda b, g, pt, j: (layer, pt[b, g * PAGES_PER_STEP + j], 0, 0), j=j))
            for j in range(n)]


def _idx_sample(page_table, qi_s, w_s, kin_s, cache_kidx_t, layer, past, n_new, n_rep):
    bd = qi_s.shape[0]
    lp = past + LANES
    n_steps = past // (PAGES_PER_STEP * LANES)
    topk = min(TOPK_MAX, (past + n_new) // 4)
    idx_bits = (past + n_new - 1).bit_length()
    per_b = lambda r, c: pl.BlockSpec((None, r, c), lambda b, g, pt: (b, 0, 0))
    return pl.pallas_call(
        functools.partial(_idx_sample_kernel, past=past, topk=topk, idx_bits=idx_bits, n_rep=n_rep),
        grid_spec=pltpu.PrefetchScalarGridSpec(
            num_scalar_prefetch=1, grid=(bd, n_steps),
            in_specs=[per_b(IDX_HEADS * SUBLANES, IDX_DIM), per_b(SUBLANES, LANES), per_b(LANES, IDX_DIM)]
            + _page_specs(IDX_DIM, layer, PAGES_PER_STEP),
            out_specs=per_b(n_rep * SUBLANES, lp),
            scratch_shapes=[pltpu.VMEM((SUBLANES, lp), I32)]),
        out_shape=jax.ShapeDtypeStruct((bd, n_rep * SUBLANES, lp), F32),
        compiler_params=_cparams(("parallel", "arbitrary")),
        name="idx_sample",
    )(page_table, qi_s, w_s, kin_s, *([cache_kidx_t] * PAGES_PER_STEP))


def _attn_sample_kernel(pt_ref, *refs, past, use_mask, diff_scale):
    del pt_ref
    refs = list(refs)
    lam_ref = refs.pop(0) if diff_scale is not None else None
    q_ref = refs.pop(0)
    mb_ref = refs.pop(0) if use_mask else None
    bias_ref, kn_ref, vn_ref = refs[:3]
    refs = refs[3:]
    g_ref = refs.pop(0) if diff_scale is not None else None
    kpages = refs[:PAGES_PER_STEP]
    vpages = refs[PAGES_PER_STEP:2 * PAGES_PER_STEP]
    o_ref, m_sc, l_sc, acc_sc = refs[2 * PAGES_PER_STEP:]
    g = pl.program_id(1)
    ck = PAGES_PER_STEP * LANES

    @pl.when(g == 0)
    def _():
        m_sc[...] = jnp.full(m_sc.shape, NEG, F32)
        l_sc[...] = jnp.zeros(l_sc.shape, F32)
        acc_sc[...] = jnp.zeros(acc_sc.shape, F32)

    def attend(lg, c0, width, pv):
        lg = lg + bias_ref[:, pl.ds(c0, width)]
        if use_mask:
            lg = lg + mb_ref[:, pl.ds(c0, width)]
        m_old = m_sc[...]
        m_new = jnp.maximum(m_old, jnp.max(lg, axis=1, keepdims=True))
        alpha = jnp.exp2(m_old - m_new)
        p = jnp.exp2(lg - m_new)
        l_sc[...] = alpha * l_sc[...] + jnp.sum(p, axis=1, keepdims=True)
        acc_sc[...] = alpha * acc_sc[...] + pv(p.astype(BF16))
        m_sc[...] = m_new

    q = q_ref[...]
    vt = _gather_chunk(vpages)
    attend(_dot(q, _gather_chunk(kpages)), pl.multiple_of(g * ck, ck), ck, lambda p: _dot_t(p, vt))

    @pl.when(g == pl.num_programs(1) - 1)
    def _():
        attend(_dot_t(q, kn_ref[...]), past, LANES, lambda p: _dot(p, vn_ref[...]))
        head_of_col = lax.broadcasted_iota(I32, (SUBLANES, D_A), 1) // HEAD_DIM
        n_comp = q.shape[0] // (N_HEADS_A * SUBLANES)

        def gather_heads(comp):
            out = jnp.zeros((SUBLANES, D_A), F32)
            for h in range(N_HEADS_A):
                r0 = (h * n_comp + comp) * SUBLANES
                o_h = acc_sc[r0:r0 + SUBLANES, :] / l_sc[r0:r0 + SUBLANES, :]
                out = jnp.where(head_of_col == h, o_h, out)
            return out

        if diff_scale is None:
            o_ref[...] = gather_heads(0)
        else:
            o = gather_heads(0) - lam_ref[0] * gather_heads(1)
            sq = o * o
            r = jnp.zeros((SUBLANES, D_A), F32)
            for h in range(N_HEADS_B):
                ms = jnp.sum(jnp.where(head_of_col == h, sq, 0.0), axis=1, keepdims=True) * (1.0 / HEAD_DIM)
                r = jnp.where(head_of_col == h, lax.rsqrt(ms + EPS), r)
            o_ref[...] = o * r * g_ref[...] * diff_scale


def _attn_sample(page_table, q_bd, mb, bias_s, kn_s, vn_s, cache_kt, cache_vt, layer, past,
                 lam=None, g_row=None, diff_scale=None):
    bd, rows, _ = q_bd.shape
    lp = past + LANES
    n_steps = past // (PAGES_PER_STEP * LANES)
    use_mask = mb is not None
    per_b = lambda *s: pl.BlockSpec((None,) + s, lambda b, g, pt: (b,) + (0,) * len(s))
    const = lambda *s: pl.BlockSpec(s, lambda b, g, pt: (0,) * len(s))
    args, specs = [], []
    if diff_scale is not None:
        args.append(lam)
        specs.append(pl.BlockSpec(memory_space=pltpu.SMEM))
    args.append(q_bd)
    specs.append(per_b(rows, D_A))
    if use_mask:
        args.append(mb)
        specs.append(per_b(rows, lp))
    args += [bias_s, kn_s, vn_s]
    specs += [const(rows, lp), per_b(LANES, D_A), per_b(LANES, D_A)]
    if diff_scale is not None:
        args.append(g_row)
        specs.append(const(1, D_A))
    args += [cache_kt] * PAGES_PER_STEP + [cache_vt] * PAGES_PER_STEP
    specs += _page_specs(D_A, layer, PAGES_PER_STEP) + _page_specs(D_A, layer, PAGES_PER_STEP)
    return pl.pallas_call(
        functools.partial(_attn_sample_kernel, past=past, use_mask=use_mask, diff_scale=diff_scale),
        grid_spec=pltpu.PrefetchScalarGridSpec(
            num_scalar_prefetch=1, grid=(bd, n_steps), in_specs=specs,
            out_specs=per_b(SUBLANES, D_A),
            scratch_shapes=[pltpu.VMEM((rows, 1), F32), pltpu.VMEM((rows, 1), F32),
                            pltpu.VMEM((rows, D_A), F32)]),
        out_shape=jax.ShapeDtypeStruct((bd, SUBLANES, D_A), F32),
        compiler_params=_cparams(("parallel", "arbitrary")),
        name="attn_sample_diff" if diff_scale is not None else "attn_sample_dsa",
    )(page_table, *args)


def _outproj_kernel(x_ref, oa_ref, ob_ref, u_ref, a_ref, b_ref, gb_ref, cw_ref, wo_ref, o_ref,
                    *, seq, per_row_state):
    tm = x_ref.shape[0]
    u = u_ref[...]
    row = lax.broadcasted_iota(I32, (tm, D_C), 0)
    um1 = pltpu.roll(u, 1, axis=0)
    um2 = pltpu.roll(u, 2, axis=0)
    if per_row_state:
        t = row % seq
        um1 = jnp.where(t == 0, a_ref[...], um1)
        um2 = jnp.where(t < 2, b_ref[...], um2)
    else:
        first = (pl.program_id(0) % (seq // tm)) == 0
        prev = jnp.where(first, b_ref[...], a_ref[...])
        p6 = prev[SUBLANES - 2:SUBLANES - 1, :]
        p7 = prev[SUBLANES - 1:SUBLANES, :]
        um1 = jnp.where(row == 0, p7, um1)
        um2 = jnp.where(row == 0, p6, jnp.where(row == 1, p7, um2))
    y = cw_ref[0:1, :] * um2 + cw_ref[1:2, :] * um1 + cw_ref[2:3, :] * u
    oc = (gb_ref[...] * y).astype(BF16)
    mixed = (_dot(oa_ref[...], wo_ref[0:D_A, :]) + _dot(ob_ref[...], wo_ref[D_A:D_A + D_B, :])
             + _dot(oc, wo_ref[D_A + D_B:D_MODEL, :]))
    o_ref[...] = x_ref[...] + mixed


def _outproj(x, oa, ob, u, gb, cw, wo, a, b, seq, per_row_state):
    n = x.shape[0]
    tm = min(512, n)
    assert n % tm == 0
    rowt = lambda c: pl.BlockSpec((tm, c), lambda i: (i, 0))
    if per_row_state:
        a_spec, b_spec = rowt(D_C), rowt(D_C)
    else:
        assert seq % tm == 0
        a_spec = pl.BlockSpec((SUBLANES, D_C), lambda i: (jnp.maximum(i * (tm // SUBLANES) - 1, 0), 0))
        b_spec = pl.BlockSpec((None, SUBLANES, D_C), lambda i: (i // (seq // tm), 0, 0))
    return pl.pallas_call(
        functools.partial(_outproj_kernel, seq=seq, per_row_state=per_row_state),
        grid=(n // tm,),
        in_specs=[rowt(D_MODEL), rowt(D_A), rowt(D_B), rowt(D_C), a_spec, b_spec, rowt(D_C),
                  pl.BlockSpec((SUBLANES, D_C), lambda i: (0, 0)),
                  pl.BlockSpec((D_MODEL, D_MODEL), lambda i: (0, 0))],
        out_specs=rowt(D_MODEL),
        out_shape=jax.ShapeDtypeStruct((n, D_MODEL), F32),
        compiler_params=_cparams(("parallel",)),
        name="outproj",
    )(x, oa, ob, u, a, b, gb, cw, wo)


def _silu(a):
    return a / (1.0 + jnp.exp(-a))


def _ffn_kernel(x_ref, g_ref, w1_ref, w3_ref, w2_ref, o_ref, *, n_chunks):
    x = x_ref[...]
    h = _rms(x, g_ref[...]).astype(BF16)
    tf = w1_ref.shape[1] // n_chunks
    out = x
    for c in range(n_chunks):
        a1 = _dot(h, w1_ref[:, c * tf:(c + 1) * tf])
        a3 = _dot(h, w3_ref[:, c * tf:(c + 1) * tf])
        out = out + _dot((_silu(a1) * a3).astype(BF16), w2_ref[c * tf:(c + 1) * tf, :])
    o_ref[...] = out


def _ffn(x, g, w1, w3, w2):
    n = x.shape[0]
    f = w1.shape[1]
    tm = min(256, n)
    n_chunks = 2 if f % (2 * LANES) == 0 else 1
    const = lambda s: pl.BlockSpec(s, lambda i: (0, 0), pipeline_mode=pl.Buffered(1))
    return pl.pallas_call(
        functools.partial(_ffn_kernel, n_chunks=n_chunks),
        grid=(n // tm,),
        in_specs=[pl.BlockSpec((tm, D_MODEL), lambda i: (i, 0)), pl.BlockSpec((1, D_MODEL), lambda i: (0, 0)),
                  const((D_MODEL, f)), const((D_MODEL, f)), const((f, D_MODEL))],
        out_specs=pl.BlockSpec((tm, D_MODEL), lambda i: (i, 0)),
        out_shape=jax.ShapeDtypeStruct((n, D_MODEL), F32),
        compiler_params=_cparams(("parallel",)),
        name="ffn",
    )(x, g, w1, w3, w2)


def _moe_kernel(x_ref, g_ref, wr_ref, br_ref, w1_ref, w3_ref, w2_ref, gf_ref, o_ref,
                h_sc, gate_sc, acc_sc):
    e = pl.program_id(1)
    f = pl.program_id(2)
    tm = x_ref.shape[0]
    lane = lax.broadcasted_iota(I32, (tm, LANES), 1)

    @pl.when((e == 0) & (f == 0))
    def _():
        h = _rms(x_ref[...], g_ref[...])
        h_sc[...] = h.astype(BF16)
        logits = jnp.dot(h, wr_ref[...], preferred_element_type=F32,
                         precision=lax.Precision.HIGHEST) + br_ref[...]
        lane_f = lane.astype(F32)
        top1 = jnp.max(logits, axis=1, keepdims=True)
        i1 = jnp.min(jnp.where(logits == top1, lane_f, float(LANES)), axis=1, keepdims=True)
        rest = jnp.where(lane_f == i1, NEG, logits)
        top2 = jnp.max(rest, axis=1, keepdims=True)
        i2 = jnp.min(jnp.where(rest == top2, lane_f, float(LANES)), axis=1, keepdims=True)
        e2 = jnp.exp(top2 - top1)
        den = 1.0 + e2
        gate_sc[...] = jnp.where(lane_f == i1, 1.0 / den, 0.0) + jnp.where(lane_f == i2, e2 / den, 0.0)
        acc_sc[...] = jnp.zeros(acc_sc.shape, F32)

    h = h_sc[...]
    act = (_silu(_dot(h, w1_ref[...])) * _dot(h, w3_ref[...])).astype(BF16)
    ge = jnp.sum(jnp.where(lane == e, gate_sc[...], 0.0), axis=1, keepdims=True)
    acc_sc[...] += ge * _dot(act, w2_ref[...])

    @pl.when((e == pl.num_programs(1) - 1) & (f == pl.num_programs(2) - 1))
    def _():
        o_ref[...] = _rms(x_ref[...] + acc_sc[...], gf_ref[...])


def _moe_final(x, g, wr, br, w1, w3, w2, g_final):
    n = x.shape[0]
    n_exp, _, f = w1.shape
    tm = min(512, n)
    n_chunks = 2 if f % (2 * LANES) == 0 else 1
    tf = f // n_chunks
    tok = lambda c: pl.BlockSpec((tm, c), lambda i, e, j: (i, 0))
    const = lambda r, c: pl.BlockSpec((r, c), lambda i, e, j: (0, 0))
    return pl.pallas_call(
        _moe_kernel,
        grid=(n // tm, n_exp, n_chunks),
        in_specs=[tok(D_MODEL), const(1, D_MODEL), const(D_MODEL, LANES), const(1, LANES),
                  pl.BlockSpec((None, D_MODEL, tf), lambda i, e, j: (e, 0, j)),
                  pl.BlockSpec((None, D_MODEL, tf), lambda i, e, j: (e, 0, j)),
                  pl.BlockSpec((None, tf, D_MODEL), lambda i, e, j: (e, j, 0)),
                  const(1, D_MODEL)],
        out_specs=tok(D_MODEL),
        out_shape=jax.ShapeDtypeStruct((n, D_MODEL), F32),
        scratch_shapes=[pltpu.VMEM((tm, D_MODEL), BF16), pltpu.VMEM((tm, LANES), F32),
                        pltpu.VMEM((tm, D_MODEL), F32)],
        compiler_params=_cparams(("parallel", "arbitrary", "arbitrary")),
        name="moe_final",
    )(x, g, wr, br, w1, w3, w2, g_final)


def _t5_bucket(n):
    max_exact = NUM_BUCKETS // 2
    nf = jnp.maximum(n, max_exact).astype(F32)
    large = max_exact + (jnp.log(nf / max_exact) / math.log(MAX_DISTANCE / max_exact)
                         * (NUM_BUCKETS - max_exact)).astype(I32)
    return jnp.where(n < max_exact, n, jnp.minimum(large, NUM_BUCKETS - 1))


def _bias_of_distance(rel_bias, d):
    buckets = _t5_bucket(jnp.arange(MAX_DISTANCE, dtype=I32))
    tab = (rel_bias * LOG2E).T.reshape((rel_bias.shape[1],) + (1,) * d.ndim + (NUM_BUCKETS,))
    out = jnp.broadcast_to(tab[..., 0], (rel_bias.shape[1],) + d.shape)
    for b in range(1, NUM_BUCKETS):
        out = jnp.where(d >= jnp.sum(buckets < b), tab[..., b], out)
    return jnp.where(d < 0, NEG, out)


def _prompt_tiles(rel_bias):
    i = jnp.arange(ATT_TILE, dtype=I32)
    d = i[None, :] - i[:, None]
    tiles = jnp.stack([_bias_of_distance(rel_bias, d + v * ATT_TILE) for v in range(3)])
    return tiles[:, :N_HEADS_A], tiles[:, N_HEADS_A:]


def _sample_bias(rel_bias, past, n_new):
    i = jnp.arange(SUBLANES, dtype=I32)[:, None]
    k = jnp.arange(past + LANES, dtype=I32)[None, :]
    b = _bias_of_distance(rel_bias, past + i - k)
    b = jnp.where(k < past + n_new, b, NEG)
    return b[:N_HEADS_A], b[N_HEADS_A:]


def _pad_rows(a, rows):
    pad = [(0, 0)] * a.ndim
    pad[-2] = (0, rows - a.shape[-2])
    return jnp.pad(a, pad)


def _block_diag_queries(q, n_comp):
    bd = q.shape[0]
    qp = _pad_rows(q, SUBLANES)[:, None]
    width = HEAD_DIM // n_comp
    owner = jnp.arange(D_A) // width
    blocks = jnp.arange(N_HEADS_A * n_comp)
    keep = (owner[None, :] == blocks[:, None])[None, :, None, :]
    return jnp.where(keep, qp, jnp.zeros_like(qp)).reshape(bd, N_HEADS_A * n_comp * SUBLANES, D_A)


def _layer_weights(l, w_in, w_out, conv_w, subln, lam_q1, lam_k1, lam_q2, lam_k2):
    w = w_in[l]
    qa, ka, va, qi, kw, qb, kb, vb, hc, gb, gc = (
        w[:, a:b] for a, b in ((0, 384), (384, 768), (768, 1152), (1152, 1408), (1408, 1476),
                               (1476, 1860), (1860, 2244), (2244, 2628), (2628, 2884), (2884, 3140),
                               (3140, 3396)))
    kw = jnp.pad(kw, ((0, 0), (0, LANES - kw.shape[1])))
    w_tok = jnp.concatenate([qa, qi, qb, ka, kb, kw, hc, gb, gc], axis=1).astype(BF16)
    w_feat = jnp.concatenate([ka, va, kb, vb, kw], axis=1).T.astype(BF16)
    lam_init = 0.8 - 0.6 * math.exp(-0.3 * l)
    lam = (jnp.exp(jnp.sum(lam_q1[l] * lam_k1[l])) - jnp.exp(jnp.sum(lam_q2[l] * lam_k2[l])) + lam_init)
    return dict(w_tok=w_tok, w_feat=w_feat, w_out=w_out[l].astype(BF16), cw=_pad_rows(conv_w[l], SUBLANES),
                g_col=jnp.tile(subln[l], 2).reshape(LANES, 1), g_row=jnp.tile(subln[l], N_HEADS_B).reshape(1, D_B),
                lam=lam.reshape(1).astype(F32), out_scale=1.0 - lam_init)


def kernel(x_prompt, x_sample, cache_a_k, cache_a_v, cache_a_kidx, cache_b_k, cache_b_v, state_conv,
           page_table, w_in, w_out, norm_mix, norm_ffn, norm_final, rel_bias, lam_q1, lam_k1, lam_q2,
           lam_k2, subln, conv_w, ffn_w1, ffn_w3, ffn_w2, moe_router, moe_router_b, moe_w1, moe_w3,
           moe_w2):
    depth = w_in.shape[0]
    assert depth == 2, "layer 0 is dense, layer 1 is MoE and is followed by the final norm"
    bsz, seq, _ = x_prompt.shape
    bd, n_new, _ = x_sample.shape
    n_pool = cache_a_k.shape[1]
    past = page_table.shape[1] * LANES
    n_s = bd * n_new
    assert cache_a_k.shape[2] == LANES and n_new <= SUBLANES

    lw = [_layer_weights(l, w_in, w_out, conv_w, subln, lam_q1, lam_k1, lam_q2, lam_k2) for l in range(depth)]
    ffn = (ffn_w1[0].astype(BF16), ffn_w3[0].astype(BF16), ffn_w2[0].astype(BF16))
    moe = (jnp.pad(moe_router[0], ((0, 0), (0, LANES - N_EXPERTS))),
           jnp.pad(moe_router_b[0], (0, LANES - N_EXPERTS), constant_values=NEG).reshape(1, LANES),
           moe_w1[0].astype(BF16), moe_w3[0].astype(BF16), moe_w2[0].astype(BF16))
    row = lambda v: v.reshape(1, -1)
    tiles_a, tiles_b = _prompt_tiles(rel_bias)
    sbias_a, sbias_b = _sample_bias(rel_bias, past, n_new)
    lp = past + LANES
    sbias_a = sbias_a.reshape(N_HEADS_A * SUBLANES, lp)
    sbias_b = jnp.repeat(sbias_b, 2, axis=0).reshape(2 * N_HEADS_B * SUBLANES, lp)
    feat_major = lambda c: jnp.moveaxis(c, 2, -1).reshape(depth, n_pool, -1, LANES)
    ca_k, ca_v, ca_i, cb_k, cb_v = (feat_major(c) for c in (cache_a_k, cache_a_v, cache_a_kidx, cache_b_k, cache_b_v))

    def ffn_block(l, x):
        if l == 0:
            return _ffn(x, row(norm_ffn[l]), *ffn)
        return _moe_final(x, row(norm_ffn[l]), *moe, row(norm_final))

    def rows_of(pr, b, t):
        def tok_major(a, *s):
            c = a.shape[1]
            return jnp.moveaxis(jnp.moveaxis(a, 1, 0).reshape(c, b, t), 0, -1).reshape(b, t, *s)
        return (tok_major(pr["kat"], N_HEADS_A, HEAD_DIM), tok_major(pr["vat"], N_HEADS_A, HEAD_DIM),
                tok_major(pr["kwt"][:, :IDX_DIM], IDX_DIM), tok_major(pr["kbt"], N_HEADS_B, HEAD_DIM),
                tok_major(pr["vbt"], N_HEADS_B, HEAD_DIM),
                pr["u"].reshape(b, t, D_C)[:, t - (CONV_WIDTH - 1):])

    x = x_prompt.reshape(bsz * seq, D_MODEL)
    rows_p = []
    zeros_state = jnp.zeros((bsz, SUBLANES, D_C), F32)
    for l in range(depth):
        pr = _proj(x, row(norm_mix[l]), lw[l]["w_tok"], lw[l]["w_feat"], bsz, seq)
        rows_p.append(rows_of(pr, bsz, seq))
        oa = _dsa_prompt(pr, tiles_a, bsz, seq)
        ob = _diff_prompt(pr, tiles_b, lw[l]["lam"], lw[l]["g_col"], lw[l]["out_scale"], bsz, seq)
        x = _outproj(x, oa, ob, pr["u"], pr["gb"], lw[l]["cw"], lw[l]["w_out"], pr["u"], zeros_state,
                     seq, False)
        x = ffn_block(l, x)
    y_prompt = x.reshape(bsz, seq, D_MODEL)

    x = x_sample.reshape(n_s, D_MODEL)
    rows_s = []
    for l in range(depth):
        pr = _proj(x, row(norm_mix[l]), lw[l]["w_tok"], lw[l]["w_feat"], 1, n_s)
        rows_s.append(rows_of(pr, bd, n_new))
        r3 = lambda a: a.reshape(bd, n_new, a.shape[-1])
        new_rows = lambda a: _pad_rows(r3(a), LANES)
        new_rows_t = lambda a: _pad_rows(jnp.moveaxis(a[0].reshape(-1, bd, n_new), 0, -1), LANES).astype(BF16)
        qi_s = _pad_rows(r3(pr["qi"]).reshape(bd, n_new, IDX_HEADS, IDX_DIM).transpose(0, 2, 1, 3),
                         SUBLANES).reshape(bd, IDX_HEADS * SUBLANES, IDX_DIM)
        w_tok = jnp.moveaxis(pr["kwt"][0, IDX_DIM:IDX_DIM + IDX_HEADS].reshape(IDX_HEADS, bd, n_new), 0, -1)
        w_s = jnp.pad(_pad_rows(w_tok, SUBLANES) * IDX_HEADS ** -0.5, ((0, 0), (0, 0), (0, LANES - IDX_HEADS)))
        mb = _idx_sample(page_table, qi_s, w_s, new_rows(pr["kib"]), ca_i, l, past, n_new, N_HEADS_A)
        oa = _attn_sample(page_table, _block_diag_queries(r3(pr["qa"]), 1), mb, sbias_a,
                          new_rows(pr["kab"]), new_rows_t(pr["vatb"]), ca_k, ca_v, l, past)
        ob = _attn_sample(page_table, _block_diag_queries(r3(pr["qb"]), 2), None, sbias_b,
                          new_rows(pr["kbb"]), new_rows_t(pr["vbtb"]), cb_k, cb_v, l, past,
                          lam=lw[l]["lam"], g_row=lw[l]["g_row"], diff_scale=lw[l]["out_scale"])
        flat = lambda o: o[:, :n_new].reshape(n_s, o.shape[-1]).astype(BF16)
        st = state_conv[l]
        zero = jnp.zeros((bd, n_new - 1, D_C), F32)
        a = jnp.concatenate([st[:, 1:2], zero], axis=1).reshape(n_s, D_C)
        b = jnp.concatenate([st, zero[:, 1:]], axis=1).reshape(n_s, D_C)
        x = _outproj(x, flat(oa), flat(ob), pr["u"], pr["gb"], lw[l]["cw"], lw[l]["w_out"], a, b,
                     n_new, True)
        x = ffn_block(l, x)
    y_sample = x.reshape(bd, n_new, D_MODEL)

    stack = lambda rows: tuple(jnp.stack([r[i] for r in rows]) for i in range(6))
    return (y_prompt, y_sample) + stack(rows_p) + stack(rows_s)
```

```python
import functools
import math

import jax
import jax.numpy as jnp
from jax import lax
from jax.experimental import pallas as pl
from jax.experimental.pallas import tpu as pltpu

F32 = jnp.float32
BF16 = jnp.bfloat16
I32 = jnp.int32

D_MODEL = 1024
HEAD_DIM = 64
N_HEADS_A = 6
N_HEADS_B = 6
D_A = N_HEADS_A * HEAD_DIM
D_B = N_HEADS_B * HEAD_DIM
D_C = D_MODEL - D_A - D_B
DIFF_DIM = HEAD_DIM // 2
IDX_HEADS = 4
IDX_DIM = 64
TOPK_MAX = 256
CONV_WIDTH = 3
NUM_BUCKETS = 32
MAX_DISTANCE = 128
N_EXPERTS = 8
EPS = 1e-6

LANES = 128
SUBLANES = 8
VMEM_LIMIT = 56 * 1024 * 1024
LOG2E = 1.4426950408889634
NEG = -1e30
INT_MIN = -2**31
BIG_IDX = 2**30
I16 = jnp.int16
MIN16 = -2**15
BIG16 = 2**15 - 1
PACK16 = 16

ATT_TILE = 256
PAGES_PER_STEP = 16
N_PAIRS = D_A // LANES

R_QA, R_QI, R_QB, R_KA, R_KB, R_KI, R_HC, R_GB, R_GC, R_END = (
    0, 384, 640, 1024, 1408, 1792, 1920, 2176, 2432, 2688)
T_KA, T_VA, T_KB, T_VB, T_KW, T_END = 0, 384, 768, 1152, 1536, 1664


def _cparams(sem):
    return pltpu.CompilerParams(dimension_semantics=sem, vmem_limit_bytes=VMEM_LIMIT)


def _dot_t(a, b):
    return lax.dot_general(a, b, (((1,), (1,)), ((), ())), preferred_element_type=F32)


def _dot(a, b):
    return jnp.dot(a, b, preferred_element_type=F32)


def _rms(x, g):
    ms = jnp.mean(x * x, axis=-1, keepdims=True)
    return x * lax.rsqrt(ms + EPS) * g


def _proj_kernel(x_ref, g_ref, w_ref, wt_ref, qa_ref, qi_ref, qb_ref, kab_ref, kbb_ref, kib_ref,
                 u_ref, gb_ref, kat_ref, vat_ref, vatb_ref, kbt_ref, vbt_ref, vbtb_ref, kwt_ref):
    h = _rms(x_ref[...], g_ref[...]).astype(BF16)

    def mm(a, b):
        return _dot(h, w_ref[:, a:b])

    def mt(a, b):
        return _dot_t(wt_ref[a:b, :], h)

    qa_ref[...] = (mm(R_QA, R_QI) * (HEAD_DIM ** -0.5 * LOG2E)).astype(BF16)
    qi_ref[...] = (mm(R_QI, R_QB) * IDX_DIM ** -0.5).astype(BF16)
    qb_ref[...] = (mm(R_QB, R_KA) * (DIFF_DIM ** -0.5 * LOG2E)).astype(BF16)
    kab_ref[...] = mm(R_KA, R_KB).astype(BF16)
    kbb_ref[...] = mm(R_KB, R_KI).astype(BF16)
    kib_ref[...] = mm(R_KI, R_HC)[:, :IDX_DIM].astype(BF16)
    hc = mm(R_HC, R_GB)
    gb_ref[...] = mm(R_GB, R_GC)
    u_ref[...] = mm(R_GC, R_END) * hc
    kat_ref[...] = mt(T_KA, T_VA)
    va = mt(T_VA, T_KB)
    vat_ref[...] = va
    vatb_ref[...] = va.astype(BF16)
    kbt_ref[...] = mt(T_KB, T_VB)
    vb = mt(T_VB, T_KW)
    vbt_ref[...] = vb
    vbtb_ref[...] = vb.astype(BF16)
    kwt_ref[...] = mt(T_KW, T_END)


def _proj(x, g, w, wt, bsz, t):
    tm = min(ATT_TILE, t)
    assert t % tm == 0
    tpb = t // tm
    tok = dict(qa=(D_A, BF16), qi=(IDX_HEADS * IDX_DIM, BF16), qb=(D_B, BF16), kab=(D_A, BF16),
               kbb=(D_B, BF16), kib=(IDX_DIM, BF16), u=(D_C, F32), gb=(D_C, F32))
    feat = dict(kat=(D_A, F32), vat=(D_A, F32), vatb=(D_A, BF16), kbt=(D_B, F32), vbt=(D_B, F32),
                vbtb=(D_B, BF16), kwt=(LANES, F32))
    outs = pl.pallas_call(
        _proj_kernel,
        grid=(bsz * tpb,),
        in_specs=[pl.BlockSpec((tm, D_MODEL), lambda i: (i, 0)),
                  pl.BlockSpec((1, D_MODEL), lambda i: (0, 0)),
                  pl.BlockSpec((D_MODEL, R_END), lambda i: (0, 0)),
                  pl.BlockSpec((T_END, D_MODEL), lambda i: (0, 0))],
        out_specs=[pl.BlockSpec((tm, c), lambda i: (i, 0)) for c, _ in tok.values()]
        + [pl.BlockSpec((None, c, tm), lambda i: (i // tpb, 0, i % tpb)) for c, _ in feat.values()],
        out_shape=[jax.ShapeDtypeStruct((bsz * t, c), d) for c, d in tok.values()]
        + [jax.ShapeDtypeStruct((bsz, c, t), d) for c, d in feat.values()],
        compiler_params=_cparams(("parallel",)),
        name="proj",
    )(x, g, w, wt)
    return dict(zip(list(tok) + list(feat), outs))


def _sortable_key(score):
    score = jnp.where(score == 0.0, 0.0, score)
    bits = pltpu.bitcast(score, I32)
    return bits ^ ((bits >> 31) & 0x7FFFFFFF)


def _topk_search(count, transform, shape, topk, idx_bits):
    kf = float(topk)
    t = jnp.where(count(lambda blk, c: blk >= 0, None) >= kf, 0, INT_MIN).astype(I32)

    def value_bit(i, t):
        cand = t | lax.shift_left(jnp.int32(1), 30 - i)
        return jnp.where(count(lambda blk, c: blk >= c, cand) >= kf, cand, t)

    t = lax.fori_loop(0, 31, value_bit, t)
    transform(jnp.maximum(t, INT_MIN + 1))

    def index_bit(i, m):
        cand = m | lax.shift_left(jnp.int32(1), idx_bits - 1 - i)
        return jnp.where(count(lambda blk, c: blk < c, cand) < kf, cand, m)

    return lax.fori_loop(0, idx_bits, index_bit, jnp.zeros(shape, I32))


def _rewrite_keys(blk, t, idx):
    return jnp.where(blk > t, -1, jnp.where(blk == t, idx, BIG_IDX))


def _tree_sum(parts):
    while len(parts) > 1:
        parts = [a + b for a, b in zip(parts[::2], parts[1::2])] + parts[len(parts) & ~1:]
    return parts[0]


def _topk_search_16(hi_ref, lo_ref, n_ch, tk, topk, idx_bits):
    tq = hi_ref.shape[1]
    one, zero = jnp.int16(1), jnp.int16(0)
    kf = float(topk)
    row = lax.broadcasted_iota(I32, (tk, tq), 0)

    def chunk(kc):
        return pl.ds(pl.multiple_of(kc * tk, tk), tk)

    def count(ref, pred, cand):
        c16 = jnp.broadcast_to(cand, (PACK16, tq)).astype(I16)

        def body(kc, acc):
            blk = ref[chunk(kc), :]
            return acc + _tree_sum([jnp.where(pred(blk[g * PACK16:(g + 1) * PACK16], c16), one, zero)
                                    for g in range(tk // PACK16)])

        acc = lax.fori_loop(0, n_ch, body, jnp.zeros((PACK16, tq), I16))
        return jnp.sum(acc.astype(F32), axis=0, keepdims=True)

    def largest_with(ref, need):
        ge = lambda c: count(ref, lambda b, c16: b >= c16, c)
        t = jnp.where(ge(jnp.zeros((1, tq), I32)) >= need, 0, MIN16).astype(I32)
        return lax.fori_loop(
            0, 15, lambda i, t: jnp.where(ge(t | lax.shift_left(jnp.int32(1), 14 - i)) >= need,
                                          t | lax.shift_left(jnp.int32(1), 14 - i), t), t)

    def full16(v):
        return jnp.broadcast_to(v, (tk, tq)).astype(I16)

    t_hi = jnp.maximum(largest_with(hi_ref, kf), MIN16 + 1)
    need = kf - count(hi_ref, lambda b, c16: b > c16, t_hi)
    thi_f = full16(t_hi)

    def keep_candidates(kc, c):
        lo_ref[chunk(kc), :] = jnp.where(hi_ref[chunk(kc), :] == thi_f, lo_ref[chunk(kc), :], jnp.int16(MIN16))
        return c

    lax.fori_loop(0, n_ch, keep_candidates, 0)
    tlo_f = full16(largest_with(lo_ref, need))
    minus1, big = jnp.int16(-1), jnp.int16(BIG16)

    def rewrite(kc, c):
        hi = hi_ref[chunk(kc), :]
        lo = lo_ref[chunk(kc), :]
        idx = (kc * tk + row).astype(I16)
        inner = jnp.where(lo > tlo_f, minus1, jnp.where(lo == tlo_f, idx, big))
        hi_ref[chunk(kc), :] = jnp.where(hi > thi_f, minus1, jnp.where(hi == thi_f, inner, big))
        return c

    lax.fori_loop(0, n_ch, rewrite, 0)

    def index_bit(i, m):
        cand = m | lax.shift_left(jnp.int32(1), idx_bits - 1 - i)
        return jnp.where(count(hi_ref, lambda b, c16: b < c16, cand) < kf, cand, m)

    return lax.fori_loop(0, idx_bits, index_bit, jnp.zeros((1, tq), I32))


def _softmax_step_t(lg_of, vt, m_ref, l_ref, acc_ref):
    for j in range(m_ref.shape[-1] // LANES):
        cols = slice(j * LANES, (j + 1) * LANES)
        lg = lg_of(cols)
        m_old = m_ref[:, cols]
        m_new = jnp.maximum(m_old, jnp.max(lg, axis=0, keepdims=True))
        alpha = jnp.exp2(m_old - m_new)
        p = jnp.exp2(lg - m_new)
        l_ref[:, cols] = alpha * l_ref[:, cols] + jnp.sum(p, axis=0, keepdims=True)
        acc_ref[:, cols] = alpha * acc_ref[:, cols] + _dot(vt, p.astype(BF16))
        m_ref[:, cols] = m_new


def _dsa_prompt_kernel(qi_ref, kwt_ref, kib_ref, qa_ref, kab_ref, vat_ref, tiles_ref, oa_ref,
                       hi_sc, lo_sc, mb_sc, lg_sc, qm_sc, m_sc, l_sc, acc_sc, *, topk, idx_bits):
    tq = ATT_TILE
    qi = pl.program_id(1)
    n_ch = qi + 1
    q0 = qi * tq
    row = lax.broadcasted_iota(I32, (tq, tq), 0)
    col = lax.broadcasted_iota(I32, (tq, tq), 1)

    w4 = kwt_ref[IDX_DIM:IDX_DIM + SUBLANES, :] * IDX_HEADS ** -0.5

    def score_chunk(kc, c):
        k0 = pl.multiple_of(kc * tq, tq)
        kch = kib_ref[pl.ds(k0, tq), :]
        sc = jnp.zeros((tq, tq), F32)
        for h in range(IDX_HEADS):
            s = _dot_t(kch, qi_ref[:, h * IDX_DIM:(h + 1) * IDX_DIM])
            sc = sc + w4[h:h + 1, :] * jnp.maximum(s, 0.0)
        valid = (k0 + row) <= (q0 + col)
        key = jnp.where(valid, _sortable_key(sc), INT_MIN)
        hi_sc[pl.ds(k0, tq), :] = (key >> 16).astype(I16)
        lo_sc[pl.ds(k0, tq), :] = ((key & 0xFFFF) + MIN16).astype(I16)
        return c

    lax.fori_loop(0, n_ch, score_chunk, 0)

    thr = _topk_search_16(hi_sc, lo_sc, n_ch, tq, topk, idx_bits)

    lane = lax.broadcasted_iota(I32, (tq, LANES), 1)
    for p in range(N_PAIRS):
        qp = qa_ref[:, p * LANES:(p + 1) * LANES]
        qm_sc[p, 0:tq, :] = jnp.where(lane < HEAD_DIM, qp, jnp.zeros_like(qp))
        qm_sc[p, tq:2 * tq, :] = jnp.where(lane >= HEAD_DIM, qp, jnp.zeros_like(qp))
    m_sc[...] = jnp.full(m_sc.shape, NEG, F32)
    l_sc[...] = jnp.zeros(l_sc.shape, F32)
    acc_sc[...] = jnp.zeros(acc_sc.shape, F32)

    def logits_into(slot, kc):
        k0 = pl.multiple_of(kc * tq, tq)
        for p in range(N_PAIRS):
            kp = kab_ref[pl.ds(k0, tq), p * LANES:(p + 1) * LANES]
            lg_sc[slot, :, 2 * p * tq:2 * (p + 1) * tq] = _dot_t(kp, qm_sc[p])

    logits_into(0, 0)

    def attend(kc, c):
        k0 = pl.multiple_of(kc * tq, tq)
        slot = kc % 2
        var = jnp.where(kc == qi, 0, jnp.where(kc == qi - 1, 1, 2))
        mb_sc[...] = jnp.where(hi_sc[pl.ds(k0, tq), :].astype(I32) <= thr, 0.0, NEG)
        for h in range(N_HEADS_A):
            p = h // 2
            vt = vat_ref[p * LANES:(p + 1) * LANES, pl.ds(k0, tq)]

            def lg_of(cols, h=h):
                shifted = slice(h * tq + cols.start, h * tq + cols.stop)
                return lg_sc[slot, :, shifted] + tiles_ref[var, h, :, cols] + mb_sc[:, cols]

            _softmax_step_t(lg_of, vt, m_sc.at[h], l_sc.at[h], acc_sc.at[h])
        logits_into(1 - slot, jnp.minimum(kc + 1, n_ch - 1))
        return c

    lax.fori_loop(0, n_ch, attend, 0)

    srow = lax.broadcasted_iota(I32, (LANES, tq), 0)
    for p in range(N_PAIRS):
        o_lo = acc_sc[2 * p] / l_sc[2 * p]
        o_hi = acc_sc[2 * p + 1] / l_sc[2 * p + 1]
        o_t = jnp.where(srow < HEAD_DIM, o_lo, o_hi)
        oa_ref[:, p * LANES:(p + 1) * LANES] = o_t.T.astype(BF16)


def _dsa_prompt(pr, tiles, bsz, t):
    tq = ATT_TILE
    assert t % tq == 0
    topk = min(TOPK_MAX, t // 4)
    idx_bits = max(1, (t - 1).bit_length())
    r3 = lambda a: a.reshape(bsz, t, a.shape[-1])
    qtile = lambda c: pl.BlockSpec((None, tq, c), lambda b, i: (b, i, 0))
    full = lambda c: pl.BlockSpec((None, t, c), lambda b, i: (b, 0, 0))
    out = pl.pallas_call(
        functools.partial(_dsa_prompt_kernel, topk=topk, idx_bits=idx_bits),
        grid=(bsz, t // tq),
        in_specs=[qtile(IDX_HEADS * IDX_DIM),
                  pl.BlockSpec((None, LANES, tq), lambda b, i: (b, 0, i)),
                  full(IDX_DIM), qtile(D_A), full(D_A),
                  pl.BlockSpec((None, D_A, t), lambda b, i: (b, 0, 0)),
                  pl.BlockSpec((3, N_HEADS_A, tq, tq), lambda b, i: (0, 0, 0, 0))],
        out_specs=qtile(D_A),
        out_shape=jax.ShapeDtypeStruct((bsz, t, D_A), BF16),
        scratch_shapes=[pltpu.VMEM((t, tq), I16), pltpu.VMEM((t, tq), I16), pltpu.VMEM((tq, tq), F32),
                        pltpu.VMEM((2, tq, N_HEADS_A * tq), F32),
                        pltpu.VMEM((N_PAIRS, 2 * tq, LANES), BF16),
                        pltpu.VMEM((N_HEADS_A, 1, tq), F32), pltpu.VMEM((N_HEADS_A, 1, tq), F32),
                        pltpu.VMEM((N_HEADS_A, LANES, tq), F32)],
        compiler_params=_cparams(("parallel", "arbitrary")),
        name="dsa_prompt",
    )(r3(pr["qi"]), pr["kwt"], r3(pr["kib"]), r3(pr["qa"]), r3(pr["kab"]), pr["vatb"], tiles)
    return out.reshape(bsz * t, D_A)


def _diff_prompt_kernel(lam_ref, qb_ref, kbb_ref, vbt_ref, tiles_ref, g_ref, ob_ref,
                        lg_sc, qm_sc, m_sc, l_sc, acc_sc, *, out_scale):
    tq = ATT_TILE
    qi = pl.program_id(1)
    lane = lax.broadcasted_iota(I32, (tq, LANES), 1)
    for p in range(N_PAIRS):
        qp = qb_ref[:, p * LANES:(p + 1) * LANES]
        for v in range(4):
            inside = (lane >= v * DIFF_DIM) & (lane < (v + 1) * DIFF_DIM)
            qm_sc[p, v * tq:(v + 1) * tq, :] = jnp.where(inside, qp, jnp.zeros_like(qp))
    m_sc[...] = jnp.full(m_sc.shape, NEG, F32)
    l_sc[...] = jnp.zeros(l_sc.shape, F32)
    acc_sc[...] = jnp.zeros(acc_sc.shape, F32)

    n_ch = qi + 1

    def logits_into(slot, kc):
        k0 = pl.multiple_of(kc * tq, tq)
        for p in range(N_PAIRS):
            kp = kbb_ref[pl.ds(k0, tq), p * LANES:(p + 1) * LANES]
            for hh in range(2):
                u0 = (4 * p + 2 * hh) * tq
                lg_sc[slot, :, u0:u0 + 2 * tq] = _dot_t(kp, qm_sc[p, 2 * hh * tq:2 * (hh + 1) * tq, :])

    logits_into(0, 0)

    def attend(kc, c):
        k0 = pl.multiple_of(kc * tq, tq)
        slot = kc % 2
        var = jnp.where(kc == qi, 0, jnp.where(kc == qi - 1, 1, 2))
        for p in range(N_PAIRS):
            vt = vbt_ref[p * LANES:(p + 1) * LANES, pl.ds(k0, tq)]
            for v in range(4):
                def lg_of(cols, u=4 * p + v, h=2 * p + v // 2):
                    shifted = slice(u * tq + cols.start, u * tq + cols.stop)
                    return lg_sc[slot, :, shifted] + tiles_ref[var, h, :, cols]

                _softmax_step_t(lg_of, vt, m_sc.at[p, v], l_sc.at[p, v], acc_sc.at[p, v])
        logits_into(1 - slot, jnp.minimum(kc + 1, n_ch - 1))
        return c

    lax.fori_loop(0, n_ch, attend, 0)

    lam = lam_ref[0]
    lo = lax.broadcasted_iota(I32, (LANES, tq), 0) < HEAD_DIM
    for p in range(N_PAIRS):
        def head(v1, v2):
            return acc_sc[p, v1] / l_sc[p, v1] - lam * (acc_sc[p, v2] / l_sc[p, v2])

        o = jnp.where(lo, head(0, 1), head(2, 3))
        sq = o * o
        ms_lo = jnp.sum(jnp.where(lo, sq, 0.0), axis=0, keepdims=True) * (1.0 / HEAD_DIM)
        ms_hi = jnp.sum(jnp.where(lo, 0.0, sq), axis=0, keepdims=True) * (1.0 / HEAD_DIM)
        r = jnp.where(lo, lax.rsqrt(ms_lo + EPS), lax.rsqrt(ms_hi + EPS))
        y = o * r * g_ref[...] * out_scale
        ob_ref[:, p * LANES:(p + 1) * LANES] = y.T.astype(BF16)


def _diff_prompt(pr, tiles, lam, g_col, out_scale, bsz, t):
    tq = ATT_TILE
    r3 = lambda a: a.reshape(bsz, t, a.shape[-1])
    qtile = lambda c: pl.BlockSpec((None, tq, c), lambda b, i: (b, i, 0))
    full = lambda c: pl.BlockSpec((None, t, c), lambda b, i: (b, 0, 0))
    out = pl.pallas_call(
        functools.partial(_diff_prompt_kernel, out_scale=out_scale),
        grid=(bsz, t // tq),
        in_specs=[pl.BlockSpec(memory_space=pltpu.SMEM), qtile(D_B), full(D_B),
                  pl.BlockSpec((None, D_B, t), lambda b, i: (b, 0, 0)),
                  pl.BlockSpec((3, N_HEADS_B, tq, tq), lambda b, i: (0, 0, 0, 0)),
                  pl.BlockSpec((LANES, 1), lambda b, i: (0, 0))],
        out_specs=qtile(D_B),
        out_shape=jax.ShapeDtypeStruct((bsz, t, D_B), BF16),
        scratch_shapes=[pltpu.VMEM((2, tq, 2 * N_HEADS_B * tq), F32),
                        pltpu.VMEM((N_PAIRS, 4 * tq, LANES), BF16),
                        pltpu.VMEM((N_PAIRS, 4, 1, tq), F32), pltpu.VMEM((N_PAIRS, 4, 1, tq), F32),
                        pltpu.VMEM((N_PAIRS, 4, LANES, tq), F32)],
        compiler_params=_cparams(("parallel", "arbitrary")),
        name="diff_prompt",
    )(lam, r3(pr["qb"]), r3(pr["kbb"]), pr["vbtb"], tiles, g_col)
    return out.reshape(bsz * t, D_B)


def _gather_chunk(page_refs):
    return jnp.concatenate([r[...] for r in page_refs], axis=1).astype(BF16)


def _idx_sample_kernel(pt_ref, qi_ref, w_ref, kin_ref, *rest, past, topk, idx_bits, n_rep):
    del pt_ref
    pages = rest[:PAGES_PER_STEP]
    mb_ref, s_sc = rest[PAGES_PER_STEP:]
    g = pl.program_id(1)
    ck = PAGES_PER_STEP * LANES
    w4 = w_ref[...]

    def scores(s_all):
        sc = jnp.zeros((SUBLANES, s_all.shape[1]), F32)
        for h in range(IDX_HEADS):
            sc = sc + w4[:, h:h + 1] * jnp.maximum(s_all[h * SUBLANES:(h + 1) * SUBLANES], 0.0)
        return _sortable_key(sc)

    s_sc[:, pl.ds(pl.multiple_of(g * ck, ck), ck)] = scores(_dot(qi_ref[...], _gather_chunk(pages)))

    @pl.when(g == pl.num_programs(1) - 1)
    def _():
        row = lax.broadcasted_iota(I32, (SUBLANES, LANES), 0)
        col = lax.broadcasted_iota(I32, (SUBLANES, LANES), 1)
        new = scores(_dot_t(qi_ref[...], kin_ref[...]))
        s_sc[:, past:past + LANES] = jnp.where(col <= row, new, INT_MIN)
        n_groups = past // LANES + 1

        def group(i):
            return slice(i * LANES, (i + 1) * LANES)

        def count(pred, cand):
            cb = None if cand is None else jnp.broadcast_to(cand, (SUBLANES, LANES))
            parts = [jnp.where(pred(s_sc[:, group(i)], cb), 1, 0) for i in range(n_groups)]
            while len(parts) > 1:
                parts = [a + b for a, b in zip(parts[::2], parts[1::2])] + parts[len(parts) & ~1:]
            return jnp.sum(parts[0].astype(F32), axis=1, keepdims=True)

        def transform(t):
            for i in range(n_groups):
                s_sc[:, group(i)] = _rewrite_keys(s_sc[:, group(i)], t, i * LANES + col)

        m = _topk_search(count, transform, (SUBLANES, 1), topk, idx_bits)
        mb = jnp.where(s_sc[...] <= m, 0.0, NEG)
        mb_ref[...] = jnp.concatenate([mb] * n_rep, axis=0)


def _page_specs(width, layer, n):
    return [pl.BlockSpec((None, None, width, LANES),
                         functools.partial(lambda b, g, pt, j: (layer, pt[b, g * PAGES_PER_STEP + j], 0, 0), j=j))
            for j in range(n)]


def _idx_sample(page_table, qi_s, w_s, kin_s, cache_kidx_t, layer, past, n_new, n_rep):
    bd = qi_s.shape[0]
    lp = past + LANES
    n_steps = past // (PAGES_PER_STEP * LANES)
    topk = min(TOPK_MAX, (past + n_new) // 4)
    idx_bits = (past + n_new - 1).bit_length()
    per_b = lambda r, c: pl.BlockSpec((None, r, c), lambda b, g, pt: (b, 0, 0))
    return pl.pallas_call(
        functools.partial(_idx_sample_kernel, past=past, topk=topk, idx_bits=idx_bits, n_rep=n_rep),
        grid_spec=pltpu.PrefetchScalarGridSpec(
            num_scalar_prefetch=1, grid=(bd, n_steps),
            in_specs=[per_b(IDX_HEADS * SUBLANES, IDX_DIM), per_b(SUBLANES, LANES), per_b(LANES, IDX_DIM)]
            + _page_specs(IDX_DIM, layer, PAGES_PER_STEP),
            out_specs=per_b(n_rep * SUBLANES, lp),
            scratch_shapes=[pltpu.VMEM((SUBLANES, lp), I32)]),
        out_shape=jax.ShapeDtypeStruct((bd, n_rep * SUBLANES, lp), F32),
        compiler_params=_cparams(("parallel", "arbitrary")),
        name="idx_sample",
    )(page_table, qi_s, w_s, kin_s, *([cache_kidx_t] * PAGES_PER_STEP))


def _attn_sample_kernel(pt_ref, *refs, past, use_mask, diff_scale):
    del pt_ref
    refs = list(refs)
    lam_ref = refs.pop(0) if diff_scale is not None else None
    q_ref = refs.pop(0)
    mb_ref = refs.pop(0) if use_mask else None
    bias_ref, kn_ref, vn_ref = refs[:3]
    refs = refs[3:]
    g_ref = refs.pop(0) if diff_scale is not None else None
    kpages = refs[:PAGES_PER_STEP]
    vpages = refs[PAGES_PER_STEP:2 * PAGES_PER_STEP]
    o_ref, m_sc, l_sc, acc_sc = refs[2 * PAGES_PER_STEP:]
    g = pl.program_id(1)
    ck = PAGES_PER_STEP * LANES

    @pl.when(g == 0)
    def _():
        m_sc[...] = jnp.full(m_sc.shape, NEG, F32)
        l_sc[...] = jnp.zeros(l_sc.shape, F32)
        acc_sc[...] = jnp.zeros(acc_sc.shape, F32)

    def attend(lg, c0, width, pv):
        lg = lg + bias_ref[:, pl.ds(c0, width)]
        if use_mask:
            lg = lg + mb_ref[:, pl.ds(c0, width)]
        m_old = m_sc[...]
        m_new = jnp.maximum(m_old, jnp.max(lg, axis=1, keepdims=True))
        alpha = jnp.exp2(m_old - m_new)
        p = jnp.exp2(lg - m_new)
        l_sc[...] = alpha * l_sc[...] + jnp.sum(p, axis=1, keepdims=True)
        acc_sc[...] = alpha * acc_sc[...] + pv(p.astype(BF16))
        m_sc[...] = m_new

    q = q_ref[...]
    vt = _gather_chunk(vpages)
    attend(_dot(q, _gather_chunk(kpages)), pl.multiple_of(g * ck, ck), ck, lambda p: _dot_t(p, vt))

    @pl.when(g == pl.num_programs(1) - 1)
    def _():
        attend(_dot_t(q, kn_ref[...]), past, LANES, lambda p: _dot(p, vn_ref[...]))
        head_of_col = lax.broadcasted_iota(I32, (SUBLANES, D_A), 1) // HEAD_DIM
        n_comp = q.shape[0] // (N_HEADS_A * SUBLANES)

        def gather_heads(comp):
            out = jnp.zeros((SUBLANES, D_A), F32)
            for h in range(N_HEADS_A):
                r0 = (h * n_comp + comp) * SUBLANES
                o_h = acc_sc[r0:r0 + SUBLANES, :] / l_sc[r0:r0 + SUBLANES, :]
                out = jnp.where(head_of_col == h, o_h, out)
            return out

        if diff_scale is None:
            o_ref[...] = gather_heads(0)
        else:
            o = gather_heads(0) - lam_ref[0] * gather_heads(1)
            sq = o * o
            r = jnp.zeros((SUBLANES, D_A), F32)
            for h in range(N_HEADS_B):
                ms = jnp.sum(jnp.where(head_of_col == h, sq, 0.0), axis=1, keepdims=True) * (1.0 / HEAD_DIM)
                r = jnp.where(head_of_col == h, lax.rsqrt(ms + EPS), r)
            o_ref[...] = o * r * g_ref[...] * diff_scale


def _attn_sample(page_table, q_bd, mb, bias_s, kn_s, vn_s, cache_kt, cache_vt, layer, past,
                 lam=None, g_row=None, diff_scale=None):
    bd, rows, _ = q_bd.shape
    lp = past + LANES
    n_steps = past // (PAGES_PER_STEP * LANES)
    use_mask = mb is not None
    per_b = lambda *s: pl.BlockSpec((None,) + s, lambda b, g, pt: (b,) + (0,) * len(s))
    const = lambda *s: pl.BlockSpec(s, lambda b, g, pt: (0,) * len(s))
    args, specs = [], []
    if diff_scale is not None:
        args.append(lam)
        specs.append(pl.BlockSpec(memory_space=pltpu.SMEM))
    args.append(q_bd)
    specs.append(per_b(rows, D_A))
    if use_mask:
        args.append(mb)
        specs.append(per_b(rows, lp))
    args += [bias_s, kn_s, vn_s]
    specs += [const(rows, lp), per_b(LANES, D_A), per_b(LANES, D_A)]
    if diff_scale is not None:
        args.append(g_row)
        specs.append(const(1, D_A))
    args += [cache_kt] * PAGES_PER_STEP + [cache_vt] * PAGES_PER_STEP
    specs += _page_specs(D_A, layer, PAGES_PER_STEP) + _page_specs(D_A, layer, PAGES_PER_STEP)
    return pl.pallas_call(
        functools.partial(_attn_sample_kernel, past=past, use_mask=use_mask, diff_scale=diff_scale),
        grid_spec=pltpu.PrefetchScalarGridSpec(
            num_scalar_prefetch=1, grid=(bd, n_steps), in_specs=specs,
            out_specs=per_b(SUBLANES, D_A),
            scratch_shapes=[pltpu.VMEM((rows, 1), F32), pltpu.VMEM((rows, 1), F32),
                            pltpu.VMEM((rows, D_A), F32)]),
        out_shape=jax.ShapeDtypeStruct((bd, SUBLANES, D_A), F32),
        compiler_params=_cparams(("parallel", "arbitrary")),
        name="attn_sample_diff" if diff_scale is not None else "attn_sample_dsa",
    )(page_table, *args)


def _outproj_kernel(x_ref, oa_ref, ob_ref, u_ref, a_ref, b_ref, gb_ref, cw_ref, wo_ref, o_ref,
                    *, seq, per_row_state):
    tm = x_ref.shape[0]
    u = u_ref[...]
    row = lax.broadcasted_iota(I32, (tm, D_C), 0)
    um1 = pltpu.roll(u, 1, axis=0)
    um2 = pltpu.roll(u, 2, axis=0)
    if per_row_state:
        t = row % seq
        um1 = jnp.where(t == 0, a_ref[...], um1)
        um2 = jnp.where(t < 2, b_ref[...], um2)
    else:
        first = (pl.program_id(0) % (seq // tm)) == 0
        prev = jnp.where(first, b_ref[...], a_ref[...])
        p6 = prev[SUBLANES - 2:SUBLANES - 1, :]
        p7 = prev[SUBLANES - 1:SUBLANES, :]
        um1 = jnp.where(row == 0, p7, um1)
        um2 = jnp.where(row == 0, p6, jnp.where(row == 1, p7, um2))
    y = cw_ref[0:1, :] * um2 + cw_ref[1:2, :] * um1 + cw_ref[2:3, :] * u
    oc = (gb_ref[...] * y).astype(BF16)
    mixed = (_dot(oa_ref[...], wo_ref[0:D_A, :]) + _dot(ob_ref[...], wo_ref[D_A:D_A + D_B, :])
             + _dot(oc, wo_ref[D_A + D_B:D_MODEL, :]))
    o_ref[...] = x_ref[...] + mixed


def _outproj(x, oa, ob, u, gb, cw, wo, a, b, seq, per_row_state):
    n = x.shape[0]
    tm = min(512, n)
    assert n % tm == 0
    rowt = lambda c: pl.BlockSpec((tm, c), lambda i: (i, 0))
    if per_row_state:
        a_spec, b_spec = rowt(D_C), rowt(D_C)
    else:
        assert seq % tm == 0
        a_spec = pl.BlockSpec((SUBLANES, D_C), lambda i: (jnp.maximum(i * (tm // SUBLANES) - 1, 0), 0))
        b_spec = pl.BlockSpec((None, SUBLANES, D_C), lambda i: (i // (seq // tm), 0, 0))
    return pl.pallas_call(
        functools.partial(_outproj_kernel, seq=seq, per_row_state=per_row_state),
        grid=(n // tm,),
        in_specs=[rowt(D_MODEL), rowt(D_A), rowt(D_B), rowt(D_C), a_spec, b_spec, rowt(D_C),
                  pl.BlockSpec((SUBLANES, D_C), lambda i: (0, 0)),
                  pl.BlockSpec((D_MODEL, D_MODEL), lambda i: (0, 0))],
        out_specs=rowt(D_MODEL),
        out_shape=jax.ShapeDtypeStruct((n, D_MODEL), F32),
        compiler_params=_cparams(("parallel",)),
        name="outproj",
    )(x, oa, ob, u, a, b, gb, cw, wo)


def _silu(a):
    return a / (1.0 + jnp.exp(-a))


def _ffn_kernel(x_ref, g_ref, w1_ref, w3_ref, w2_ref, o_ref, *, n_chunks):
    x = x_ref[...]
    h = _rms(x, g_ref[...]).astype(BF16)
    tf = w1_ref.shape[1] // n_chunks
    out = x
    for c in range(n_chunks):
        a1 = _dot(h, w1_ref[:, c * tf:(c + 1) * tf])
        a3 = _dot(h, w3_ref[:, c * tf:(c + 1) * tf])
        out = out + _dot((_silu(a1) * a3).astype(BF16), w2_ref[c * tf:(c + 1) * tf, :])
    o_ref[...] = out


def _ffn(x, g, w1, w3, w2):
    n = x.shape[0]
    f = w1.shape[1]
    tm = min(256, n)
    n_chunks = 2 if f % (2 * LANES) == 0 else 1
    const = lambda s: pl.BlockSpec(s, lambda i: (0, 0), pipeline_mode=pl.Buffered(1))
    return pl.pallas_call(
        functools.partial(_ffn_kernel, n_chunks=n_chunks),
        grid=(n // tm,),
        in_specs=[pl.BlockSpec((tm, D_MODEL), lambda i: (i, 0)), pl.BlockSpec((1, D_MODEL), lambda i: (0, 0)),
                  const((D_MODEL, f)), const((D_MODEL, f)), const((f, D_MODEL))],
        out_specs=pl.BlockSpec((tm, D_MODEL), lambda i: (i, 0)),
        out_shape=jax.ShapeDtypeStruct((n, D_MODEL), F32),
        compiler_params=_cparams(("parallel",)),
        name="ffn",
    )(x, g, w1, w3, w2)


def _moe_kernel(x_ref, g_ref, wr_ref, br_ref, w1_ref, w3_ref, w2_ref, gf_ref, o_ref,
                h_sc, gate_sc, acc_sc):
    e = pl.program_id(1)
    f = pl.program_id(2)
    tm = x_ref.shape[0]
    lane = lax.broadcasted_iota(I32, (tm, LANES), 1)

    @pl.when((e == 0) & (f == 0))
    def _():
        h = _rms(x_ref[...], g_ref[...])
        h_sc[...] = h.astype(BF16)
        logits = jnp.dot(h, wr_ref[...], preferred_element_type=F32,
                         precision=lax.Precision.HIGHEST) + br_ref[...]
        lane_f = lane.astype(F32)
        top1 = jnp.max(logits, axis=1, keepdims=True)
        i1 = jnp.min(jnp.where(logits == top1, lane_f, float(LANES)), axis=1, keepdims=True)
        rest = jnp.where(lane_f == i1, NEG, logits)
        top2 = jnp.max(rest, axis=1, keepdims=True)
        i2 = jnp.min(jnp.where(rest == top2, lane_f, float(LANES)), axis=1, keepdims=True)
        e2 = jnp.exp(top2 - top1)
        den = 1.0 + e2
        gate_sc[...] = jnp.where(lane_f == i1, 1.0 / den, 0.0) + jnp.where(lane_f == i2, e2 / den, 0.0)
        acc_sc[...] = jnp.zeros(acc_sc.shape, F32)

    h = h_sc[...]
    act = (_silu(_dot(h, w1_ref[...])) * _dot(h, w3_ref[...])).astype(BF16)
    ge = jnp.sum(jnp.where(lane == e, gate_sc[...], 0.0), axis=1, keepdims=True)
    acc_sc[...] += ge * _dot(act, w2_ref[...])

    @pl.when((e == pl.num_programs(1) - 1) & (f == pl.num_programs(2) - 1))
    def _():
        o_ref[...] = _rms(x_ref[...] + acc_sc[...], gf_ref[...])


def _moe_final(x, g, wr, br, w1, w3, w2, g_final):
    n = x.shape[0]
    n_exp, _, f = w1.shape
    tm = min(512, n)
    n_chunks = 2 if f % (2 * LANES) == 0 else 1
    tf = f // n_chunks
    tok = lambda c: pl.BlockSpec((tm, c), lambda i, e, j: (i, 0))
    const = lambda r, c: pl.BlockSpec((r, c), lambda i, e, j: (0, 0))
    return pl.pallas_call(
        _moe_kernel,
        grid=(n // tm, n_exp, n_chunks),
        in_specs=[tok(D_MODEL), const(1, D_MODEL), const(D_MODEL, LANES), const(1, LANES),
                  pl.BlockSpec((None, D_MODEL, tf), lambda i, e, j: (e, 0, j)),
                  pl.BlockSpec((None, D_MODEL, tf), lambda i, e, j: (e, 0, j)),
                  pl.BlockSpec((None, tf, D_MODEL), lambda i, e, j: (e, j, 0)),
                  const(1, D_MODEL)],
        out_specs=tok(D_MODEL),
        out_shape=jax.ShapeDtypeStruct((n, D_MODEL), F32),
        scratch_shapes=[pltpu.VMEM((tm, D_MODEL), BF16), pltpu.VMEM((tm, LANES), F32),
                        pltpu.VMEM((tm, D_MODEL), F32)],
        compiler_params=_cparams(("parallel", "arbitrary", "arbitrary")),
        name="moe_final",
    )(x, g, wr, br, w1, w3, w2, g_final)


def _t5_bucket(n):
    max_exact = NUM_BUCKETS // 2
    nf = jnp.maximum(n, max_exact).astype(F32)
    large = max_exact + (jnp.log(nf / max_exact) / math.log(MAX_DISTANCE / max_exact)
                         * (NUM_BUCKETS - max_exact)).astype(I32)
    return jnp.where(n < max_exact, n, jnp.minimum(large, NUM_BUCKETS - 1))


def _bias_of_distance(rel_bias, d):
    buckets = _t5_bucket(jnp.arange(MAX_DISTANCE, dtype=I32))
    tab = (rel_bias * LOG2E).T.reshape((rel_bias.shape[1],) + (1,) * d.ndim + (NUM_BUCKETS,))
    out = jnp.broadcast_to(tab[..., 0], (rel_bias.shape[1],) + d.shape)
    for b in range(1, NUM_BUCKETS):
        out = jnp.where(d >= jnp.sum(buckets < b), tab[..., b], out)
    return jnp.where(d < 0, NEG, out)


def _prompt_tiles(rel_bias):
    i = jnp.arange(ATT_TILE, dtype=I32)
    d = i[None, :] - i[:, None]
    tiles = jnp.stack([_bias_of_distance(rel_bias, d + v * ATT_TILE) for v in range(3)])
    return tiles[:, :N_HEADS_A], tiles[:, N_HEADS_A:]


def _sample_bias(rel_bias, past, n_new):
    i = jnp.arange(SUBLANES, dtype=I32)[:, None]
    k = jnp.arange(past + LANES, dtype=I32)[None, :]
    b = _bias_of_distance(rel_bias, past + i - k)
    b = jnp.where(k < past + n_new, b, NEG)
    return b[:N_HEADS_A], b[N_HEADS_A:]


def _pad_rows(a, rows):
    pad = [(0, 0)] * a.ndim
    pad[-2] = (0, rows - a.shape[-2])
    return jnp.pad(a, pad)


def _block_diag_queries(q, n_comp):
    bd = q.shape[0]
    qp = _pad_rows(q, SUBLANES)[:, None]
    width = HEAD_DIM // n_comp
    owner = jnp.arange(D_A) // width
    blocks = jnp.arange(N_HEADS_A * n_comp)
    keep = (owner[None, :] == blocks[:, None])[None, :, None, :]
    return jnp.where(keep, qp, jnp.zeros_like(qp)).reshape(bd, N_HEADS_A * n_comp * SUBLANES, D_A)


def _layer_weights(l, w_in, w_out, conv_w, subln, lam_q1, lam_k1, lam_q2, lam_k2):
    w = w_in[l]
    qa, ka, va, qi, kw, qb, kb, vb, hc, gb, gc = (
        w[:, a:b] for a, b in ((0, 384), (384, 768), (768, 1152), (1152, 1408), (1408, 1476),
                               (1476, 1860), (1860, 2244), (2244, 2628), (2628, 2884), (2884, 3140),
                               (3140, 3396)))
    kw = jnp.pad(kw, ((0, 0), (0, LANES - kw.shape[1])))
    w_tok = jnp.concatenate([qa, qi, qb, ka, kb, kw, hc, gb, gc], axis=1).astype(BF16)
    w_feat = jnp.concatenate([ka, va, kb, vb, kw], axis=1).T.astype(BF16)
    lam_init = 0.8 - 0.6 * math.exp(-0.3 * l)
    lam = (jnp.exp(jnp.sum(lam_q1[l] * lam_k1[l])) - jnp.exp(jnp.sum(lam_q2[l] * lam_k2[l])) + lam_init)
    return dict(w_tok=w_tok, w_feat=w_feat, w_out=w_out[l].astype(BF16), cw=_pad_rows(conv_w[l], SUBLANES),
                g_col=jnp.tile(subln[l], 2).reshape(LANES, 1), g_row=jnp.tile(subln[l], N_HEADS_B).reshape(1, D_B),
                lam=lam.reshape(1).astype(F32), out_scale=1.0 - lam_init)


def kernel(x_prompt, x_sample, cache_a_k, cache_a_v, cache_a_kidx, cache_b_k, cache_b_v, state_conv,
           page_table, w_in, w_out, norm_mix, norm_ffn, norm_final, rel_bias, lam_q1, lam_k1, lam_q2,
           lam_k2, subln, conv_w, ffn_w1, ffn_w3, ffn_w2, moe_router, moe_router_b, moe_w1, moe_w3,
           moe_w2):
    depth = w_in.shape[0]
    assert depth == 2, "layer 0 is dense, layer 1 is MoE and is followed by the final norm"
    bsz, seq, _ = x_prompt.shape
    bd, n_new, _ = x_sample.shape
    n_pool = cache_a_k.shape[1]
    past = page_table.shape[1] * LANES
    n_s = bd * n_new
    assert cache_a_k.shape[2] == LANES and n_new <= SUBLANES

    lw = [_layer_weights(l, w_in, w_out, conv_w, subln, lam_q1, lam_k1, lam_q2, lam_k2) for l in range(depth)]
    ffn = (ffn_w1[0].astype(BF16), ffn_w3[0].astype(BF16), ffn_w2[0].astype(BF16))
    moe = (jnp.pad(moe_router[0], ((0, 0), (0, LANES - N_EXPERTS))),
           jnp.pad(moe_router_b[0], (0, LANES - N_EXPERTS), constant_values=NEG).reshape(1, LANES),
           moe_w1[0].astype(BF16), moe_w3[0].astype(BF16), moe_w2[0].astype(BF16))
    row = lambda v: v.reshape(1, -1)
    tiles_a, tiles_b = _prompt_tiles(rel_bias)
    sbias_a, sbias_b = _sample_bias(rel_bias, past, n_new)
    lp = past + LANES
    sbias_a = sbias_a.reshape(N_HEADS_A * SUBLANES, lp)
    sbias_b = jnp.repeat(sbias_b, 2, axis=0).reshape(2 * N_HEADS_B * SUBLANES, lp)
    feat_major = lambda c: jnp.moveaxis(c, 2, -1).reshape(depth, n_pool, -1, LANES)
    ca_k, ca_v, ca_i, cb_k, cb_v = (feat_major(c) for c in (cache_a_k, cache_a_v, cache_a_kidx, cache_b_k, cache_b_v))

    def ffn_block(l, x):
        if l == 0:
            return _ffn(x, row(norm_ffn[l]), *ffn)
        return _moe_final(x, row(norm_ffn[l]), *moe, row(norm_final))

    def rows_of(pr, b, t):
        def tok_major(a, *s):
            c = a.shape[1]
            return jnp.moveaxis(jnp.moveaxis(a, 1, 0).reshape(c, b, t), 0, -1).reshape(b, t, *s)
        return (tok_major(pr["kat"], N_HEADS_A, HEAD_DIM), tok_major(pr["vat"], N_HEADS_A, HEAD_DIM),
                tok_major(pr["kwt"][:, :IDX_DIM], IDX_DIM), tok_major(pr["kbt"], N_HEADS_B, HEAD_DIM),
                tok_major(pr["vbt"], N_HEADS_B, HEAD_DIM),
                pr["u"].reshape(b, t, D_C)[:, t - (CONV_WIDTH - 1):])

    x = x_prompt.reshape(bsz * seq, D_MODEL)
    rows_p = []
    zeros_state = jnp.zeros((bsz, SUBLANES, D_C), F32)
    for l in range(depth):
        pr = _proj(x, row(norm_mix[l]), lw[l]["w_tok"], lw[l]["w_feat"], bsz, seq)
        rows_p.append(rows_of(pr, bsz, seq))
        oa = _dsa_prompt(pr, tiles_a, bsz, seq)
        ob = _diff_prompt(pr, tiles_b, lw[l]["lam"], lw[l]["g_col"], lw[l]["out_scale"], bsz, seq)
        x = _outproj(x, oa, ob, pr["u"], pr["gb"], lw[l]["cw"], lw[l]["w_out"], pr["u"], zeros_state,
                     seq, False)
        x = ffn_block(l, x)
    y_prompt = x.reshape(bsz, seq, D_MODEL)

    x = x_sample.reshape(n_s, D_MODEL)
    rows_s = []
    for l in range(depth):
        pr = _proj(x, row(norm_mix[l]), lw[l]["w_tok"], lw[l]["w_feat"], 1, n_s)
        rows_s.append(rows_of(pr, bd, n_new))
        r3 = lambda a: a.reshape(bd, n_new, a.shape[-1])
        new_rows = lambda a: _pad_rows(r3(a), LANES)
        new_rows_t = lambda a: _pad_rows(jnp.moveaxis(a[0].reshape(-1, bd, n_new), 0, -1), LANES).astype(BF16)
        qi_s = _pad_rows(r3(pr["qi"]).reshape(bd, n_new, IDX_HEADS, IDX_DIM).transpose(0, 2, 1, 3),
                         SUBLANES).reshape(bd, IDX_HEADS * SUBLANES, IDX_DIM)
        w_tok = jnp.moveaxis(pr["kwt"][0, IDX_DIM:IDX_DIM + IDX_HEADS].reshape(IDX_HEADS, bd, n_new), 0, -1)
        w_s = jnp.pad(_pad_rows(w_tok, SUBLANES) * IDX_HEADS ** -0.5, ((0, 0), (0, 0), (0, LANES - IDX_HEADS)))
        mb = _idx_sample(page_table, qi_s, w_s, new_rows(pr["kib"]), ca_i, l, past, n_new, N_HEADS_A)
        oa = _attn_sample(page_table, _block_diag_queries(r3(pr["qa"]), 1), mb, sbias_a,
                          new_rows(pr["kab"]), new_rows_t(pr["vatb"]), ca_k, ca_v, l, past)
        ob = _attn_sample(page_table, _block_diag_queries(r3(pr["qb"]), 2), None, sbias_b,
                          new_rows(pr["kbb"]), new_rows_t(pr["vbtb"]), cb_k, cb_v, l, past,
                          lam=lw[l]["lam"], g_row=lw[l]["g_row"], diff_scale=lw[l]["out_scale"])
        flat = lambda o: o[:, :n_new].reshape(n_s, o.shape[-1]).astype(BF16)
        st = state_conv[l]
        zero = jnp.zeros((bd, n_new - 1, D_C), F32)
        a = jnp.concatenate([st[:, 1:2], zero], axis=1).reshape(n_s, D_C)
        b = jnp.concatenate([st, zero[:, 1:]], axis=1).reshape(n_s, D_C)
        x = _outproj(x, flat(oa), flat(ob), pr["u"], pr["gb"], lw[l]["cw"], lw[l]["w_out"], a, b,
                     n_new, True)
        x = ffn_block(l, x)
    y_sample = x.reshape(bd, n_new, D_MODEL)

    stack = lambda rows: tuple(jnp.stack([r[i] for r in rows]) for i in range(6))
    return (y_prompt, y_sample) + stack(rows_p) + stack(rows_s)
```

```python
import functools
import math

import jax
import jax.numpy as jnp
from jax import lax
from jax.experimental import pallas as pl
from jax.experimental.pallas import tpu as pltpu

F32 = jnp.float32
BF16 = jnp.bfloat16
I32 = jnp.int32

D_MODEL = 1024
HEAD_DIM = 64
N_HEADS_A = 6
N_HEADS_B = 6
D_A = N_HEADS_A * HEAD_DIM
D_B = N_HEADS_B * HEAD_DIM
D_C = D_MODEL - D_A - D_B
DIFF_DIM = HEAD_DIM // 2
IDX_HEADS = 4
IDX_DIM = 64
TOPK_MAX = 256
CONV_WIDTH = 3
NUM_BUCKETS = 32
MAX_DISTANCE = 128
N_EXPERTS = 8
EPS = 1e-6

LANES = 128
SUBLANES = 8
VMEM_LIMIT = 56 * 1024 * 1024
LOG2E = 1.4426950408889634
NEG = -1e30
INT_MIN = -2**31
BIG_IDX = 2**30
I16 = jnp.int16
MIN16 = -2**15
BIG16 = 2**15 - 1
PACK16 = 16

ATT_TILE = 256
PAGES_PER_STEP = 16
N_PAIRS = D_A // LANES

R_QA, R_QI, R_QB, R_KA, R_KB, R_KI, R_HC, R_GB, R_GC, R_END = (
    0, 384, 640, 1024, 1408, 1792, 1920, 2176, 2432, 2688)
T_KA, T_VA, T_KB, T_VB, T_KW, T_END = 0, 384, 768, 1152, 1536, 1664


def _cparams(sem):
    return pltpu.CompilerParams(dimension_semantics=sem, vmem_limit_bytes=VMEM_LIMIT)


def _dot_t(a, b):
    return lax.dot_general(a, b, (((1,), (1,)), ((), ())), preferred_element_type=F32)


def _dot(a, b):
    return jnp.dot(a, b, preferred_element_type=F32)


def _rms(x, g):
    ms = jnp.mean(x * x, axis=-1, keepdims=True)
    return x * lax.rsqrt(ms + EPS) * g


def _proj_kernel(x_ref, g_ref, w_ref, wt_ref, qa_ref, qi_ref, qb_ref, kab_ref, kbb_ref, kib_ref,
                 u_ref, gb_ref, kat_ref, vat_ref, vatb_ref, kbt_ref, vbt_ref, vbtb_ref, kwt_ref):
    h = _rms(x_ref[...], g_ref[...]).astype(BF16)

    def mm(a, b):
        return _dot(h, w_ref[:, a:b])

    def mt(a, b):
        return _dot_t(wt_ref[a:b, :], h)

    qa_ref[...] = (mm(R_QA, R_QI) * (HEAD_DIM ** -0.5 * LOG2E)).astype(BF16)
    qi_ref[...] = (mm(R_QI, R_QB) * IDX_DIM ** -0.5).astype(BF16)
    qb_ref[...] = (mm(R_QB, R_KA) * (DIFF_DIM ** -0.5 * LOG2E)).astype(BF16)
    kab_ref[...] = mm(R_KA, R_KB).astype(BF16)
    kbb_ref[...] = mm(R_KB, R_KI).astype(BF16)
    kib_ref[...] = mm(R_KI, R_HC)[:, :IDX_DIM].astype(BF16)
    hc = mm(R_HC, R_GB)
    gb_ref[...] = mm(R_GB, R_GC)
    u_ref[...] = mm(R_GC, R_END) * hc
    kat_ref[...] = mt(T_KA, T_VA)
    va = mt(T_VA, T_KB)
    vat_ref[...] = va
    vatb_ref[...] = va.astype(BF16)
    kbt_ref[...] = mt(T_KB, T_VB)
    vb = mt(T_VB, T_KW)
    vbt_ref[...] = vb
    vbtb_ref[...] = vb.astype(BF16)
    kwt_ref[...] = mt(T_KW, T_END)


def _proj(x, g, w, wt, bsz, t):
    tm = min(ATT_TILE, t)
    assert t % tm == 0
    tpb = t // tm
    tok = dict(qa=(D_A, BF16), qi=(IDX_HEADS * IDX_DIM, BF16), qb=(D_B, BF16), kab=(D_A, BF16),
               kbb=(D_B, BF16), kib=(IDX_DIM, BF16), u=(D_C, F32), gb=(D_C, F32))
    feat = dict(kat=(D_A, F32), vat=(D_A, F32), vatb=(D_A, BF16), kbt=(D_B, F32), vbt=(D_B, F32),
                vbtb=(D_B, BF16), kwt=(LANES, F32))
    outs = pl.pallas_call(
        _proj_kernel,
        grid=(bsz * tpb,),
        in_specs=[pl.BlockSpec((tm, D_MODEL), lambda i: (i, 0)),
                  pl.BlockSpec((1, D_MODEL), lambda i: (0, 0)),
                  pl.BlockSpec((D_MODEL, R_END), lambda i: (0, 0)),
                  pl.BlockSpec((T_END, D_MODEL), lambda i: (0, 0))],
        out_specs=[pl.BlockSpec((tm, c), lambda i: (i, 0)) for c, _ in tok.values()]
        + [pl.BlockSpec((None, c, tm), lambda i: (i // tpb, 0, i % tpb)) for c, _ in feat.values()],
        out_shape=[jax.ShapeDtypeStruct((bsz * t, c), d) for c, d in tok.values()]
        + [jax.ShapeDtypeStruct((bsz, c, t), d) for c, d in feat.values()],
        compiler_params=_cparams(("parallel",)),
        name="proj",
    )(x, g, w, wt)
    return dict(zip(list(tok) + list(feat), outs))


def _sortable_key(score):
    score = jnp.where(score == 0.0, 0.0, score)
    bits = pltpu.bitcast(score, I32)
    return bits ^ ((bits >> 31) & 0x7FFFFFFF)


def _topk_search(count, transform, shape, topk, idx_bits):
    kf = float(topk)
    t = jnp.where(count(lambda blk, c: blk >= 0, None) >= kf, 0, INT_MIN).astype(I32)

    def value_bit(i, t):
        cand = t | lax.shift_left(jnp.int32(1), 30 - i)
        return jnp.where(count(lambda blk, c: blk >= c, cand) >= kf, cand, t)

    t = lax.fori_loop(0, 31, value_bit, t)
    transform(jnp.maximum(t, INT_MIN + 1))

    def index_bit(i, m):
        cand = m | lax.shift_left(jnp.int32(1), idx_bits - 1 - i)
        return jnp.where(count(lambda blk, c: blk < c, cand) < kf, cand, m)

    return lax.fori_loop(0, idx_bits, index_bit, jnp.zeros(shape, I32))


def _rewrite_keys(blk, t, idx):
    return jnp.where(blk > t, -1, jnp.where(blk == t, idx, BIG_IDX))


def _tree_sum(parts):
    while len(parts) > 1:
        parts = [a + b for a, b in zip(parts[::2], parts[1::2])] + parts[len(parts) & ~1:]
    return parts[0]


def _topk_search_16(hi_ref, lo_ref, n_ch, tk, topk, idx_bits):
    tq = hi_ref.shape[1]
    one, zero = jnp.int16(1), jnp.int16(0)
    kf = float(topk)
    row = lax.broadcasted_iota(I32, (tk, tq), 0)

    def chunk(kc):
        return pl.ds(pl.multiple_of(kc * tk, tk), tk)

    def count(ref, pred, cand):
        c16 = jnp.broadcast_to(cand, (PACK16, tq)).astype(I16)

        def body(kc, acc):
            blk = ref[chunk(kc), :]
            return acc + _tree_sum([jnp.where(pred(blk[g * PACK16:(g + 1) * PACK16], c16), one, zero)
                                    for g in range(tk // PACK16)])

        acc = lax.fori_loop(0, n_ch, body, jnp.zeros((PACK16, tq), I16))
        return jnp.sum(acc.astype(F32), axis=0, keepdims=True)

    def largest_with(ref, need):
        ge = lambda c: count(ref, lambda b, c16: b >= c16, c)
        t = jnp.where(ge(jnp.zeros((1, tq), I32)) >= need, 0, MIN16).astype(I32)
        return lax.fori_loop(
            0, 15, lambda i, t: jnp.where(ge(t | lax.shift_left(jnp.int32(1), 14 - i)) >= need,
                                          t | lax.shift_left(jnp.int32(1), 14 - i), t), t)

    def full16(v):
        return jnp.broadcast_to(v, (tk, tq)).astype(I16)

    t_hi = jnp.maximum(largest_with(hi_ref, kf), MIN16 + 1)
    need = kf - count(hi_ref, lambda b, c16: b > c16, t_hi)
    thi_f = full16(t_hi)

    def keep_candidates(kc, c):
        lo_ref[chunk(kc), :] = jnp.where(hi_ref[chunk(kc), :] == thi_f, lo_ref[chunk(kc), :], jnp.int16(MIN16))
        return c

    lax.fori_loop(0, n_ch, keep_candidates, 0)
    tlo_f = full16(largest_with(lo_ref, need))
    minus1, big = jnp.int16(-1), jnp.int16(BIG16)

    def rewrite(kc, c):
        hi = hi_ref[chunk(kc), :]
        lo = lo_ref[chunk(kc), :]
        idx = (kc * tk + row).astype(I16)
        inner = jnp.where(lo > tlo_f, minus1, jnp.where(lo == tlo_f, idx, big))
        hi_ref[chunk(kc), :] = jnp.where(hi > thi_f, minus1, jnp.where(hi == thi_f, inner, big))
        return c

    lax.fori_loop(0, n_ch, rewrite, 0)

    def index_bit(i, m):
        cand = m | lax.shift_left(jnp.int32(1), idx_bits - 1 - i)
        return jnp.where(count(hi_ref, lambda b, c16: b < c16, cand) < kf, cand, m)

    return lax.fori_loop(0, idx_bits, index_bit, jnp.zeros((1, tq), I32))


def _softmax_step_t(lg_of, vt, m_ref, l_ref, acc_ref):
    alphas, probs = [], []
    for j in range(m_ref.shape[-1] // LANES):
        cols = slice(j * LANES, (j + 1) * LANES)
        lg = lg_of(cols)
        m_old = m_ref[:, cols]
        m_new = jnp.maximum(m_old, jnp.max(lg, axis=0, keepdims=True))
        alpha = jnp.exp2(m_old - m_new)
        p = jnp.exp2(lg - m_new)
        l_ref[:, cols] = alpha * l_ref[:, cols] + jnp.sum(p, axis=0, keepdims=True)
        m_ref[:, cols] = m_new
        alphas.append(alpha)
        probs.append(p.astype(BF16))
    acc_ref[...] = (jnp.concatenate(alphas, axis=1) * acc_ref[...]
                    + _dot(vt, jnp.concatenate(probs, axis=1)))


def _dsa_prompt_kernel(qi_ref, kwt_ref, kib_ref, qa_ref, kab_ref, vat_ref, tiles_ref, oa_ref,
                       hi_sc, lo_sc, mb_sc, lg_sc, qm_sc, m_sc, l_sc, acc_sc, *, topk, idx_bits):
    tq = ATT_TILE
    qi = pl.program_id(1)
    n_ch = qi + 1
    q0 = qi * tq
    row = lax.broadcasted_iota(I32, (tq, tq), 0)
    col = lax.broadcasted_iota(I32, (tq, tq), 1)

    w4 = kwt_ref[IDX_DIM:IDX_DIM + SUBLANES, :] * IDX_HEADS ** -0.5

    def score_chunk(kc, c):
        k0 = pl.multiple_of(kc * tq, tq)
        kch = kib_ref[pl.ds(k0, tq), :]
        sc = jnp.zeros((tq, tq), F32)
        for h in range(IDX_HEADS):
            s = _dot_t(kch, qi_ref[:, h * IDX_DIM:(h + 1) * IDX_DIM])
            sc = sc + w4[h:h + 1, :] * jnp.maximum(s, 0.0)
        valid = (k0 + row) <= (q0 + col)
        key = jnp.where(valid, _sortable_key(sc), INT_MIN)
        hi_sc[pl.ds(k0, tq), :] = (key >> 16).astype(I16)
        lo_sc[pl.ds(k0, tq), :] = ((key & 0xFFFF) + MIN16).astype(I16)
        return c

    lax.fori_loop(0, n_ch, score_chunk, 0)

    thr = _topk_search_16(hi_sc, lo_sc, n_ch, tq, topk, idx_bits)

    lane = lax.broadcasted_iota(I32, (tq, LANES), 1)
    for p in range(N_PAIRS):
        qp = qa_ref[:, p * LANES:(p + 1) * LANES]
        qm_sc[p, 0:tq, :] = jnp.where(lane < HEAD_DIM, qp, jnp.zeros_like(qp))
        qm_sc[p, tq:2 * tq, :] = jnp.where(lane >= HEAD_DIM, qp, jnp.zeros_like(qp))
    m_sc[...] = jnp.full(m_sc.shape, NEG, F32)
    l_sc[...] = jnp.zeros(l_sc.shape, F32)
    acc_sc[...] = jnp.zeros(acc_sc.shape, F32)

    def logits_into(slot, kc):
        k0 = pl.multiple_of(kc * tq, tq)
        for p in range(N_PAIRS):
            kp = kab_ref[pl.ds(k0, tq), p * LANES:(p + 1) * LANES]
            lg_sc[slot, :, 2 * p * tq:2 * (p + 1) * tq] = _dot_t(kp, qm_sc[p])

    logits_into(0, 0)

    def attend(kc, c):
        k0 = pl.multiple_of(kc * tq, tq)
        slot = kc % 2
        var = jnp.where(kc == qi, 0, jnp.where(kc == qi - 1, 1, 2))
        mb_sc[...] = jnp.where(hi_sc[pl.ds(k0, tq), :].astype(I32) <= thr, 0.0, NEG)
        for h in range(N_HEADS_A):
            p = h // 2
            vt = vat_ref[p * LANES:(p + 1) * LANES, pl.ds(k0, tq)]

            def lg_of(cols, h=h):
                shifted = slice(h * tq + cols.start, h * tq + cols.stop)
                return lg_sc[slot, :, shifted] + tiles_ref[var, h, :, cols] + mb_sc[:, cols]

            _softmax_step_t(lg_of, vt, m_sc.at[h], l_sc.at[h], acc_sc.at[h])
        logits_into(1 - slot, jnp.minimum(kc + 1, n_ch - 1))
        return c

    lax.fori_loop(0, n_ch, attend, 0)

    srow = lax.broadcasted_iota(I32, (LANES, tq), 0)
    for p in range(N_PAIRS):
        o_lo = acc_sc[2 * p] / l_sc[2 * p]
        o_hi = acc_sc[2 * p + 1] / l_sc[2 * p + 1]
        o_t = jnp.where(srow < HEAD_DIM, o_lo, o_hi)
        oa_ref[:, p * LANES:(p + 1) * LANES] = o_t.T.astype(BF16)


def _dsa_prompt(pr, tiles, bsz, t):
    tq = ATT_TILE
    assert t % tq == 0
    topk = min(TOPK_MAX, t // 4)
    idx_bits = max(1, (t - 1).bit_length())
    r3 = lambda a: a.reshape(bsz, t, a.shape[-1])
    qtile = lambda c: pl.BlockSpec((None, tq, c), lambda b, i: (b, i, 0))
    full = lambda c: pl.BlockSpec((None, t, c), lambda b, i: (b, 0, 0))
    out = pl.pallas_call(
        functools.partial(_dsa_prompt_kernel, topk=topk, idx_bits=idx_bits),
        grid=(bsz, t // tq),
        in_specs=[qtile(IDX_HEADS * IDX_DIM),
                  pl.BlockSpec((None, LANES, tq), lambda b, i: (b, 0, i)),
                  full(IDX_DIM), qtile(D_A), full(D_A),
                  pl.BlockSpec((None, D_A, t), lambda b, i: (b, 0, 0)),
                  pl.BlockSpec((3, N_HEADS_A, tq, tq), lambda b, i: (0, 0, 0, 0))],
        out_specs=qtile(D_A),
        out_shape=jax.ShapeDtypeStruct((bsz, t, D_A), BF16),
        scratch_shapes=[pltpu.VMEM((t, tq), I16), pltpu.VMEM((t, tq), I16), pltpu.VMEM((tq, tq), F32),
                        pltpu.VMEM((2, tq, N_HEADS_A * tq), F32),
                        pltpu.VMEM((N_PAIRS, 2 * tq, LANES), BF16),
                        pltpu.VMEM((N_HEADS_A, 1, tq), F32), pltpu.VMEM((N_HEADS_A, 1, tq), F32),
                        pltpu.VMEM((N_HEADS_A, LANES, tq), F32)],
        compiler_params=_cparams(("parallel", "arbitrary")),
        name="dsa_prompt",
    )(r3(pr["qi"]), pr["kwt"], r3(pr["kib"]), r3(pr["qa"]), r3(pr["kab"]), pr["vatb"], tiles)
    return out.reshape(bsz * t, D_A)


def _diff_prompt_kernel(lam_ref, qb_ref, kbb_ref, vbt_ref, tiles_ref, g_ref, ob_ref,
                        lg_sc, qm_sc, m_sc, l_sc, acc_sc, *, out_scale):
    tq = ATT_TILE
    qi = pl.program_id(1)
    lane = lax.broadcasted_iota(I32, (tq, LANES), 1)
    for p in range(N_PAIRS):
        qp = qb_ref[:, p * LANES:(p + 1) * LANES]
        for v in range(4):
            inside = (lane >= v * DIFF_DIM) & (lane < (v + 1) * DIFF_DIM)
            qm_sc[p, v * tq:(v + 1) * tq, :] = jnp.where(inside, qp, jnp.zeros_like(qp))
    m_sc[...] = jnp.full(m_sc.shape, NEG, F32)
    l_sc[...] = jnp.zeros(l_sc.shape, F32)
    acc_sc[...] = jnp.zeros(acc_sc.shape, F32)

    n_ch = qi + 1

    def logits_into(slot, kc):
        k0 = pl.multiple_of(kc * tq, tq)
        for p in range(N_PAIRS):
            kp = kbb_ref[pl.ds(k0, tq), p * LANES:(p + 1) * LANES]
            for hh in range(2):
                u0 = (4 * p + 2 * hh) * tq
                lg_sc[slot, :, u0:u0 + 2 * tq] = _dot_t(kp, qm_sc[p, 2 * hh * tq:2 * (hh + 1) * tq, :])

    logits_into(0, 0)

    def attend(kc, c):
        k0 = pl.multiple_of(kc * tq, tq)
        slot = kc % 2
        var = jnp.where(kc == qi, 0, jnp.where(kc == qi - 1, 1, 2))
        for p in range(N_PAIRS):
            vt = vbt_ref[p * LANES:(p + 1) * LANES, pl.ds(k0, tq)]
            for v in range(4):
                def lg_of(cols, u=4 * p + v, h=2 * p + v // 2):
                    shifted = slice(u * tq + cols.start, u * tq + cols.stop)
                    return lg_sc[slot, :, shifted] + tiles_ref[var, h, :, cols]

                _softmax_step_t(lg_of, vt, m_sc.at[p, v], l_sc.at[p, v], acc_sc.at[p, v])
        logits_into(1 - slot, jnp.minimum(kc + 1, n_ch - 1))
        return c

    lax.fori_loop(0, n_ch, attend, 0)

    lam = lam_ref[0]
    lo = lax.broadcasted_iota(I32, (LANES, tq), 0) < HEAD_DIM
    for p in range(N_PAIRS):
        def head(v1, v2):
            return acc_sc[p, v1] / l_sc[p, v1] - lam * (acc_sc[p, v2] / l_sc[p, v2])

        o = jnp.where(lo, head(0, 1), head(2, 3))
        sq = o * o
        ms_lo = jnp.sum(jnp.where(lo, sq, 0.0), axis=0, keepdims=True) * (1.0 / HEAD_DIM)
        ms_hi = jnp.sum(jnp.where(lo, 0.0, sq), axis=0, keepdims=True) * (1.0 / HEAD_DIM)
        r = jnp.where(lo, lax.rsqrt(ms_lo + EPS), lax.rsqrt(ms_hi + EPS))
        y = o * r * g_ref[...] * out_scale
        ob_ref[:, p * LANES:(p + 1) * LANES] = y.T.astype(BF16)


def _diff_prompt(pr, tiles, lam, g_col, out_scale, bsz, t):
    tq = ATT_TILE
    r3 = lambda a: a.reshape(bsz, t, a.shape[-1])
    qtile = lambda c: pl.BlockSpec((None, tq, c), lambda b, i: (b, i, 0))
    full = lambda c: pl.BlockSpec((None, t, c), lambda b, i: (b, 0, 0))
    out = pl.pallas_call(
        functools.partial(_diff_prompt_kernel, out_scale=out_scale),
        grid=(bsz, t // tq),
        in_specs=[pl.BlockSpec(memory_space=pltpu.SMEM), qtile(D_B), full(D_B),
                  pl.BlockSpec((None, D_B, t), lambda b, i: (b, 0, 0)),
                  pl.BlockSpec((3, N_HEADS_B, tq, tq), lambda b, i: (0, 0, 0, 0)),
                  pl.BlockSpec((LANES, 1), lambda b, i: (0, 0))],
        out_specs=qtile(D_B),
        out_shape=jax.ShapeDtypeStruct((bsz, t, D_B), BF16),
        scratch_shapes=[pltpu.VMEM((2, tq, 2 * N_HEADS_B * tq), F32),
                        pltpu.VMEM((N_PAIRS, 4 * tq, LANES), BF16),
                        pltpu.VMEM((N_PAIRS, 4, 1, tq), F32), pltpu.VMEM((N_PAIRS, 4, 1, tq), F32),
                        pltpu.VMEM((N_PAIRS, 4, LANES, tq), F32)],
        compiler_params=_cparams(("parallel", "arbitrary")),
        name="diff_prompt",
    )(lam, r3(pr["qb"]), r3(pr["kbb"]), pr["vbtb"], tiles, g_col)
    return out.reshape(bsz * t, D_B)


def _gather_chunk(page_refs):
    return jnp.concatenate([r[...] for r in page_refs], axis=1).astype(BF16)


def _idx_sample_kernel(pt_ref, qi_ref, w_ref, kin_ref, *rest, past, topk, idx_bits, n_rep):
    del pt_ref
    pages = rest[:PAGES_PER_STEP]
    mb_ref, s_sc = rest[PAGES_PER_STEP:]
    g = pl.program_id(1)
    ck = PAGES_PER_STEP * LANES
    w4 = w_ref[...]

    def scores(s_all):
        sc = jnp.zeros((SUBLANES, s_all.shape[1]), F32)
        for h in range(IDX_HEADS):
            sc = sc + w4[:, h:h + 1] * jnp.maximum(s_all[h * SUBLANES:(h + 1) * SUBLANES], 0.0)
        return _sortable_key(sc)

    s_sc[:, pl.ds(pl.multiple_of(g * ck, ck), ck)] = scores(_dot(qi_ref[...], _gather_chunk(pages)))

    @pl.when(g == pl.num_programs(1) - 1)
    def _():
        row = lax.broadcasted_iota(I32, (SUBLANES, LANES), 0)
        col = lax.broadcasted_iota(I32, (SUBLANES, LANES), 1)
        new = scores(_dot_t(qi_ref[...], kin_ref[...]))
        s_sc[:, past:past + LANES] = jnp.where(col <= row, new, INT_MIN)
        n_groups = past // LANES + 1

        def group(i):
            return slice(i * LANES, (i + 1) * LANES)

        def count(pred, cand):
            cb = None if cand is None else jnp.broadcast_to(cand, (SUBLANES, LANES))
            parts = [jnp.where(pred(s_sc[:, group(i)], cb), 1, 0) for i in range(n_groups)]
            while len(parts) > 1:
                parts = [a + b for a, b in zip(parts[::2], parts[1::2])] + parts[len(parts) & ~1:]
            return jnp.sum(parts[0].astype(F32), axis=1, keepdims=True)

        def transform(t):
            for i in range(n_groups):
                s_sc[:, group(i)] = _rewrite_keys(s_sc[:, group(i)], t, i * LANES + col)

        m = _topk_search(count, transform, (SUBLANES, 1), topk, idx_bits)
        mb = jnp.where(s_sc[...] <= m, 0.0, NEG)
        mb_ref[...] = jnp.concatenate([mb] * n_rep, axis=0)


def _page_specs(width, layer, n):
    return [pl.BlockSpec((None, None, width, LANES),
                         functools.partial(lambda b, g, pt, j: (layer, pt[b, g * PAGES_PER_STEP + j], 0, 0), j=j))
            for j in range(n)]


def _idx_sample(page_table, qi_s, w_s, kin_s, cache_kidx_t, layer, past, n_new, n_rep):
    bd = qi_s.shape[0]
    lp = past + LANES
    n_steps = past // (PAGES_PER_STEP * LANES)
    topk = min(TOPK_MAX, (past + n_new) // 4)
    idx_bits = (past + n_new - 1).bit_length()
    per_b = lambda r, c: pl.BlockSpec((None, r, c), lambda b, g, pt: (b, 0, 0))
    return pl.pallas_call(
        functools.partial(_idx_sample_kernel, past=past, topk=topk, idx_bits=idx_bits, n_rep=n_rep),
        grid_spec=pltpu.PrefetchScalarGridSpec(
            num_scalar_prefetch=1, grid=(bd, n_steps),
            in_specs=[per_b(IDX_HEADS * SUBLANES, IDX_DIM), per_b(SUBLANES, LANES), per_b(LANES, IDX_DIM)]
            + _page_specs(IDX_DIM, layer, PAGES_PER_STEP),
            out_specs=per_b(n_rep * SUBLANES, lp),
            scratch_shapes=[pltpu.VMEM((SUBLANES, lp), I32)]),
        out_shape=jax.ShapeDtypeStruct((bd, n_rep * SUBLANES, lp), F32),
        compiler_params=_cparams(("parallel", "arbitrary")),
        name="idx_sample",
    )(page_table, qi_s, w_s, kin_s, *([cache_kidx_t] * PAGES_PER_STEP))


def _attn_sample_kernel(pt_ref, *refs, past, use_mask, diff_scale):
    del pt_ref
    refs = list(refs)
    lam_ref = refs.pop(0) if diff_scale is not None else None
    q_ref = refs.pop(0)
    mb_ref = refs.pop(0) if use_mask else None
    bias_ref, kn_ref, vn_ref = refs[:3]
    refs = refs[3:]
    g_ref = refs.pop(0) if diff_scale is not None else None
    kpages = refs[:PAGES_PER_STEP]
    vpages = refs[PAGES_PER_STEP:2 * PAGES_PER_STEP]
    o_ref, m_sc, l_sc, acc_sc = refs[2 * PAGES_PER_STEP:]
    g = pl.program_id(1)
    ck = PAGES_PER_STEP * LANES

    @pl.when(g == 0)
    def _():
        m_sc[...] = jnp.full(m_sc.shape, NEG, F32)
        l_sc[...] = jnp.zeros(l_sc.shape, F32)
        acc_sc[...] = jnp.zeros(acc_sc.shape, F32)

    def attend(lg, c0, width, pv):
        lg = lg + bias_ref[:, pl.ds(c0, width)]
        if use_mask:
            lg = lg + mb_ref[:, pl.ds(c0, width)]
        m_old = m_sc[...]
        m_new = jnp.maximum(m_old, jnp.max(lg, axis=1, keepdims=True))
        alpha = jnp.exp2(m_old - m_new)
        p = jnp.exp2(lg - m_new)
        l_sc[...] = alpha * l_sc[...] + jnp.sum(p, axis=1, keepdims=True)
        acc_sc[...] = alpha * acc_sc[...] + pv(p.astype(BF16))
        m_sc[...] = m_new

    q = q_ref[...]
    vt = _gather_chunk(vpages)
    attend(_dot(q, _gather_chunk(kpages)), pl.multiple_of(g * ck, ck), ck, lambda p: _dot_t(p, vt))

    @pl.when(g == pl.num_programs(1) - 1)
    def _():
        attend(_dot_t(q, kn_ref[...]), past, LANES, lambda p: _dot(p, vn_ref[...]))
        head_of_col = lax.broadcasted_iota(I32, (SUBLANES, D_A), 1) // HEAD_DIM
        n_comp = q.shape[0] // (N_HEADS_A * SUBLANES)

        def gather_heads(comp):
            out = jnp.zeros((SUBLANES, D_A), F32)
            for h in range(N_HEADS_A):
                r0 = (h * n_comp + comp) * SUBLANES
                o_h = acc_sc[r0:r0 + SUBLANES, :] / l_sc[r0:r0 + SUBLANES, :]
                out = jnp.where(head_of_col == h, o_h, out)
            return out

        if diff_scale is None:
            o_ref[...] = gather_heads(0)
        else:
            o = gather_heads(0) - lam_ref[0] * gather_heads(1)
            sq = o * o
            r = jnp.zeros((SUBLANES, D_A), F32)
            for h in range(N_HEADS_B):
                ms = jnp.sum(jnp.where(head_of_col == h, sq, 0.0), axis=1, keepdims=True) * (1.0 / HEAD_DIM)
                r = jnp.where(head_of_col == h, lax.rsqrt(ms + EPS), r)
            o_ref[...] = o * r * g_ref[...] * diff_scale


def _attn_sample(page_table, q_bd, mb, bias_s, kn_s, vn_s, cache_kt, cache_vt, layer, past,
                 lam=None, g_row=None, diff_scale=None):
    bd, rows, _ = q_bd.shape
    lp = past + LANES
    n_steps = past // (PAGES_PER_STEP * LANES)
    use_mask = mb is not None
    per_b = lambda *s: pl.BlockSpec((None,) + s, lambda b, g, pt: (b,) + (0,) * len(s))
    const = lambda *s: pl.BlockSpec(s, lambda b, g, pt: (0,) * len(s))
    args, specs = [], []
    if diff_scale is not None:
        args.append(lam)
        specs.append(pl.BlockSpec(memory_space=pltpu.SMEM))
    args.append(q_bd)
    specs.append(per_b(rows, D_A))
    if use_mask:
        args.append(mb)
        specs.append(per_b(rows, lp))
    args += [bias_s, kn_s, vn_s]
    specs += [const(rows, lp), per_b(LANES, D_A), per_b(LANES, D_A)]
    if diff_scale is not None:
        args.append(g_row)
        specs.append(const(1, D_A))
    args += [cache_kt] * PAGES_PER_STEP + [cache_vt] * PAGES_PER_STEP
    specs += _page_specs(D_A, layer, PAGES_PER_STEP) + _page_specs(D_A, layer, PAGES_PER_STEP)
    return pl.pallas_call(
        functools.partial(_attn_sample_kernel, past=past, use_mask=use_mask, diff_scale=diff_scale),
        grid_spec=pltpu.PrefetchScalarGridSpec(
            num_scalar_prefetch=1, grid=(bd, n_steps), in_specs=specs,
            out_specs=per_b(SUBLANES, D_A),
            scratch_shapes=[pltpu.VMEM((rows, 1), F32), pltpu.VMEM((rows, 1), F32),
                            pltpu.VMEM((rows, D_A), F32)]),
        out_shape=jax.ShapeDtypeStruct((bd, SUBLANES, D_A), F32),
        compiler_params=_cparams(("parallel", "arbitrary")),
        name="attn_sample_diff" if diff_scale is not None else "attn_sample_dsa",
    )(page_table, *args)


def _outproj_kernel(x_ref, oa_ref, ob_ref, u_ref, a_ref, b_ref, gb_ref, cw_ref, wo_ref, o_ref,
                    *, seq, per_row_state):
    tm = x_ref.shape[0]
    u = u_ref[...]
    row = lax.broadcasted_iota(I32, (tm, D_C), 0)
    um1 = pltpu.roll(u, 1, axis=0)
    um2 = pltpu.roll(u, 2, axis=0)
    if per_row_state:
        t = row % seq
        um1 = jnp.where(t == 0, a_ref[...], um1)
        um2 = jnp.where(t < 2, b_ref[...], um2)
    else:
        first = (pl.program_id(0) % (seq // tm)) == 0
        prev = jnp.where(first, b_ref[...], a_ref[...])
        p6 = prev[SUBLANES - 2:SUBLANES - 1, :]
        p7 = prev[SUBLANES - 1:SUBLANES, :]
        um1 = jnp.where(row == 0, p7, um1)
        um2 = jnp.where(row == 0, p6, jnp.where(row == 1, p7, um2))
    y = cw_ref[0:1, :] * um2 + cw_ref[1:2, :] * um1 + cw_ref[2:3, :] * u
    oc = (gb_ref[...] * y).astype(BF16)
    mixed = (_dot(oa_ref[...], wo_ref[0:D_A, :]) + _dot(ob_ref[...], wo_ref[D_A:D_A + D_B, :])
             + _dot(oc, wo_ref[D_A + D_B:D_MODEL, :]))
    o_ref[...] = x_ref[...] + mixed


def _outproj(x, oa, ob, u, gb, cw, wo, a, b, seq, per_row_state):
    n = x.shape[0]
    tm = min(512, n)
    assert n % tm == 0
    rowt = lambda c: pl.BlockSpec((tm, c), lambda i: (i, 0))
    if per_row_state:
        a_spec, b_spec = rowt(D_C), rowt(D_C)
    else:
        assert seq % tm == 0
        a_spec = pl.BlockSpec((SUBLANES, D_C), lambda i: (jnp.maximum(i * (tm // SUBLANES) - 1, 0), 0))
        b_spec = pl.BlockSpec((None, SUBLANES, D_C), lambda i: (i // (seq // tm), 0, 0))
    return pl.pallas_call(
        functools.partial(_outproj_kernel, seq=seq, per_row_state=per_row_state),
        grid=(n // tm,),
        in_specs=[rowt(D_MODEL), rowt(D_A), rowt(D_B), rowt(D_C), a_spec, b_spec, rowt(D_C),
                  pl.BlockSpec((SUBLANES, D_C), lambda i: (0, 0)),
                  pl.BlockSpec((D_MODEL, D_MODEL), lambda i: (0, 0))],
        out_specs=rowt(D_MODEL),
        out_shape=jax.ShapeDtypeStruct((n, D_MODEL), F32),
        compiler_params=_cparams(("parallel",)),
        name="outproj",
    )(x, oa, ob, u, a, b, gb, cw, wo)


def _silu(a):
    return a / (1.0 + jnp.exp(-a))


def _ffn_kernel(x_ref, g_ref, w1_ref, w3_ref, w2_ref, o_ref, *, n_chunks):
    x = x_ref[...]
    h = _rms(x, g_ref[...]).astype(BF16)
    tf = w1_ref.shape[1] // n_chunks
    out = x
    for c in range(n_chunks):
        a1 = _dot(h, w1_ref[:, c * tf:(c + 1) * tf])
        a3 = _dot(h, w3_ref[:, c * tf:(c + 1) * tf])
        out = out + _dot((_silu(a1) * a3).astype(BF16), w2_ref[c * tf:(c + 1) * tf, :])
    o_ref[...] = out


def _ffn(x, g, w1, w3, w2):
    n = x.shape[0]
    f = w1.shape[1]
    tm = min(256, n)
    n_chunks = 2 if f % (2 * LANES) == 0 else 1
    const = lambda s: pl.BlockSpec(s, lambda i: (0, 0), pipeline_mode=pl.Buffered(1))
    return pl.pallas_call(
        functools.partial(_ffn_kernel, n_chunks=n_chunks),
        grid=(n // tm,),
        in_specs=[pl.BlockSpec((tm, D_MODEL), lambda i: (i, 0)), pl.BlockSpec((1, D_MODEL), lambda i: (0, 0)),
                  const((D_MODEL, f)), const((D_MODEL, f)), const((f, D_MODEL))],
        out_specs=pl.BlockSpec((tm, D_MODEL), lambda i: (i, 0)),
        out_shape=jax.ShapeDtypeStruct((n, D_MODEL), F32),
        compiler_params=_cparams(("parallel",)),
        name="ffn",
    )(x, g, w1, w3, w2)


def _moe_kernel(x_ref, g_ref, wr_ref, br_ref, w1_ref, w3_ref, w2_ref, gf_ref, o_ref,
                h_sc, gate_sc, rank_sc, rank_t_sc, xg_sc, og_sc, acc_sc, *, cs):
    e = pl.program_id(1)
    f = pl.program_id(2)
    tm = x_ref.shape[0]
    lane = lax.broadcasted_iota(I32, (tm, LANES), 1)

    @pl.when((e == 0) & (f == 0))
    def _():
        h = _rms(x_ref[...], g_ref[...])
        h_sc[...] = h.astype(BF16)
        logits = jnp.dot(h, wr_ref[...], preferred_element_type=F32,
                         precision=lax.Precision.HIGHEST) + br_ref[...]
        lane_f = lane.astype(F32)
        top1 = jnp.max(logits, axis=1, keepdims=True)
        i1 = jnp.min(jnp.where(logits == top1, lane_f, float(LANES)), axis=1, keepdims=True)
        rest = jnp.where(lane_f == i1, NEG, logits)
        top2 = jnp.max(rest, axis=1, keepdims=True)
        i2 = jnp.min(jnp.where(rest == top2, lane_f, float(LANES)), axis=1, keepdims=True)
        e2 = jnp.exp(top2 - top1)
        den = 1.0 + e2
        gate_sc[...] = jnp.where(lane_f == i1, 1.0 / den, 0.0) + jnp.where(lane_f == i2, e2 / den, 0.0)
        sel = (lane_f == i1) | (lane_f == i2)
        sel_b = jnp.where(sel, 1.0, 0.0).astype(BF16)
        rb = min(ATT_TILE, tm)
        for r0 in range(0, tm, rb):
            earlier = (lax.broadcasted_iota(I32, (rb, tm), 1)
                       < lax.broadcasted_iota(I32, (rb, tm), 0) + r0)
            rank_sc[r0:r0 + rb, :] = jnp.where(sel[r0:r0 + rb], _dot(jnp.where(earlier, 1.0, 0.0).astype(BF16), sel_b), -1.0)
        rank_t_sc[...] = rank_sc[...].T[0:SUBLANES, :]
        acc_sc[...] = jnp.zeros(acc_sc.shape, F32)

    rank = rank_sc[...]
    n_e = jnp.sum(jnp.where((lane == e) & (rank >= 0.0), 1.0, 0.0)).astype(I32)
    n_c = (n_e + (cs - 1)) // cs

    def slots(c):
        return pl.ds(pl.multiple_of(c * cs, cs), cs)

    @pl.when(f == 0)
    def _():
        rank_row = rank_t_sc[pl.ds(e, 1), :]
        slot = lax.broadcasted_iota(I32, (cs, tm), 0).astype(F32)

        def gather(c, carry):
            pick = jnp.where(rank_row == slot + (c * cs).astype(F32), 1.0, 0.0).astype(BF16)
            xg_sc[slots(c), :] = _dot(pick, h_sc[...]).astype(BF16)
            og_sc[slots(c), :] = jnp.zeros((cs, D_MODEL), F32)
            return carry

        lax.fori_loop(0, n_c, gather, 0)

    def expert_chunk(c, carry):
        xg = xg_sc[slots(c), :]
        act = (_silu(_dot(xg, w1_ref[...])) * _dot(xg, w3_ref[...])).astype(BF16)
        og_sc[slots(c), :] += _dot(act, w2_ref[...])
        return carry

    lax.fori_loop(0, n_c, expert_chunk, 0)

    @pl.when(f == pl.num_programs(2) - 1)
    def _():
        rank_col = jnp.sum(jnp.where(lane == e, rank, 0.0), axis=1, keepdims=True)
        gate_col = jnp.sum(jnp.where(lane == e, gate_sc[...], 0.0), axis=1, keepdims=True)
        slot = lax.broadcasted_iota(I32, (tm, cs), 1).astype(F32)

        def scatter(c, carry):
            place = jnp.where(rank_col == slot + (c * cs).astype(F32), 1.0, 0.0).astype(BF16)
            acc_sc[...] += gate_col * _dot(place, og_sc[slots(c), :].astype(BF16))
            return carry

        lax.fori_loop(0, n_c, scatter, 0)

    @pl.when((e == pl.num_programs(1) - 1) & (f == pl.num_programs(2) - 1))
    def _():
        o_ref[...] = _rms(x_ref[...] + acc_sc[...], gf_ref[...])


def _moe_final(x, g, wr, br, w1, w3, w2, g_final):
    n = x.shape[0]
    n_exp, _, f = w1.shape
    tm = min(1024, n)
    cs = min(3 * LANES, tm)
    cap = pl.cdiv(tm, cs) * cs
    n_chunks = 2 if f % (2 * LANES) == 0 else 1
    tf = f // n_chunks
    tok = lambda c: pl.BlockSpec((tm, c), lambda i, e, j: (i, 0))
    const = lambda r, c: pl.BlockSpec((r, c), lambda i, e, j: (0, 0))
    return pl.pallas_call(
        functools.partial(_moe_kernel, cs=cs),
        grid=(n // tm, n_exp, n_chunks),
        in_specs=[tok(D_MODEL), const(1, D_MODEL), const(D_MODEL, LANES), const(1, LANES),
                  pl.BlockSpec((None, D_MODEL, tf), lambda i, e, j: (e, 0, j)),
                  pl.BlockSpec((None, D_MODEL, tf), lambda i, e, j: (e, 0, j)),
                  pl.BlockSpec((None, tf, D_MODEL), lambda i, e, j: (e, j, 0)),
                  const(1, D_MODEL)],
        out_specs=tok(D_MODEL),
        out_shape=jax.ShapeDtypeStruct((n, D_MODEL), F32),
        scratch_shapes=[pltpu.VMEM((tm, D_MODEL), BF16), pltpu.VMEM((tm, LANES), F32),
                        pltpu.VMEM((tm, LANES), F32), pltpu.VMEM((SUBLANES, tm), F32),
                        pltpu.VMEM((cap, D_MODEL), BF16), pltpu.VMEM((cap, D_MODEL), F32),
                        pltpu.VMEM((tm, D_MODEL), F32)],
        compiler_params=_cparams(("parallel", "arbitrary", "arbitrary")),
        name="moe_final",
    )(x, g, wr, br, w1, w3, w2, g_final)


def _t5_bucket(n):
    max_exact = NUM_BUCKETS // 2
    nf = jnp.maximum(n, max_exact).astype(F32)
    large = max_exact + (jnp.log(nf / max_exact) / math.log(MAX_DISTANCE / max_exact)
                         * (NUM_BUCKETS - max_exact)).astype(I32)
    return jnp.where(n < max_exact, n, jnp.minimum(large, NUM_BUCKETS - 1))


def _bias_of_distance(rel_bias, d):
    buckets = _t5_bucket(jnp.arange(MAX_DISTANCE, dtype=I32))
    tab = (rel_bias * LOG2E).T.reshape((rel_bias.shape[1],) + (1,) * d.ndim + (NUM_BUCKETS,))
    out = jnp.broadcast_to(tab[..., 0], (rel_bias.shape[1],) + d.shape)
    for b in range(1, NUM_BUCKETS):
        out = jnp.where(d >= jnp.sum(buckets < b), tab[..., b], out)
    return jnp.where(d < 0, NEG, out)


def _prompt_tiles(rel_bias):
    i = jnp.arange(ATT_TILE, dtype=I32)
    d = i[None, :] - i[:, None]
    tiles = jnp.stack([_bias_of_distance(rel_bias, d + v * ATT_TILE) for v in range(3)])
    return tiles[:, :N_HEADS_A], tiles[:, N_HEADS_A:]


def _sample_bias(rel_bias, past, n_new):
    i = jnp.arange(SUBLANES, dtype=I32)[:, None]
    k = jnp.arange(past + LANES, dtype=I32)[None, :]
    b = _bias_of_distance(rel_bias, past + i - k)
    b = jnp.where(k < past + n_new, b, NEG)
    return b[:N_HEADS_A], b[N_HEADS_A:]


def _pad_rows(a, rows):
    pad = [(0, 0)] * a.ndim
    pad[-2] = (0, rows - a.shape[-2])
    return jnp.pad(a, pad)


def _block_diag_queries(q, n_comp):
    bd = q.shape[0]
    qp = _pad_rows(q, SUBLANES)[:, None]
    width = HEAD_DIM // n_comp
    owner = jnp.arange(D_A) // width
    blocks = jnp.arange(N_HEADS_A * n_comp)
    keep = (owner[None, :] == blocks[:, None])[None, :, None, :]
    return jnp.where(keep, qp, jnp.zeros_like(qp)).reshape(bd, N_HEADS_A * n_comp * SUBLANES, D_A)


def _layer_weights(l, w_in, w_out, conv_w, subln, lam_q1, lam_k1, lam_q2, lam_k2):
    w = w_in[l]
    qa, ka, va, qi, kw, qb, kb, vb, hc, gb, gc = (
        w[:, a:b] for a, b in ((0, 384), (384, 768), (768, 1152), (1152, 1408), (1408, 1476),
                               (1476, 1860), (1860, 2244), (2244, 2628), (2628, 2884), (2884, 3140),
                               (3140, 3396)))
    kw = jnp.pad(kw, ((0, 0), (0, LANES - kw.shape[1])))
    w_tok = jnp.concatenate([qa, qi, qb, ka, kb, kw, hc, gb, gc], axis=1).astype(BF16)
    w_feat = jnp.concatenate([ka, va, kb, vb, kw], axis=1).T.astype(BF16)
    lam_init = 0.8 - 0.6 * math.exp(-0.3 * l)
    lam = (jnp.exp(jnp.sum(lam_q1[l] * lam_k1[l])) - jnp.exp(jnp.sum(lam_q2[l] * lam_k2[l])) + lam_init)
    return dict(w_tok=w_tok, w_feat=w_feat, w_out=w_out[l].astype(BF16), cw=_pad_rows(conv_w[l], SUBLANES),
                g_col=jnp.tile(subln[l], 2).reshape(LANES, 1), g_row=jnp.tile(subln[l], N_HEADS_B).reshape(1, D_B),
                lam=lam.reshape(1).astype(F32), out_scale=1.0 - lam_init)


def kernel(x_prompt, x_sample, cache_a_k, cache_a_v, cache_a_kidx, cache_b_k, cache_b_v, state_conv,
           page_table, w_in, w_out, norm_mix, norm_ffn, norm_final, rel_bias, lam_q1, lam_k1, lam_q2,
           lam_k2, subln, conv_w, ffn_w1, ffn_w3, ffn_w2, moe_router, moe_router_b, moe_w1, moe_w3,
           moe_w2):
    depth = w_in.shape[0]
    assert depth == 2, "layer 0 is dense, layer 1 is MoE and is followed by the final norm"
    bsz, seq, _ = x_prompt.shape
    bd, n_new, _ = x_sample.shape
    n_pool = cache_a_k.shape[1]
    past = page_table.shape[1] * LANES
    n_s = bd * n_new
    assert cache_a_k.shape[2] == LANES and n_new <= SUBLANES

    lw = [_layer_weights(l, w_in, w_out, conv_w, subln, lam_q1, lam_k1, lam_q2, lam_k2) for l in range(depth)]
    ffn = (ffn_w1[0].astype(BF16), ffn_w3[0].astype(BF16), ffn_w2[0].astype(BF16))
    moe = (jnp.pad(moe_router[0], ((0, 0), (0, LANES - N_EXPERTS))),
           jnp.pad(moe_router_b[0], (0, LANES - N_EXPERTS), constant_values=NEG).reshape(1, LANES),
           moe_w1[0].astype(BF16), moe_w3[0].astype(BF16), moe_w2[0].astype(BF16))
    row = lambda v: v.reshape(1, -1)
    tiles_a, tiles_b = _prompt_tiles(rel_bias)
    sbias_a, sbias_b = _sample_bias(rel_bias, past, n_new)
    lp = past + LANES
    sbias_a = sbias_a.reshape(N_HEADS_A * SUBLANES, lp)
    sbias_b = jnp.repeat(sbias_b, 2, axis=0).reshape(2 * N_HEADS_B * SUBLANES, lp)
    feat_major = lambda c: jnp.moveaxis(c, 2, -1).reshape(depth, n_pool, -1, LANES)
    ca_k, ca_v, ca_i, cb_k, cb_v = (feat_major(c) for c in (cache_a_k, cache_a_v, cache_a_kidx, cache_b_k, cache_b_v))

    def ffn_block(l, x):
        if l == 0:
            return _ffn(x, row(norm_ffn[l]), *ffn)
        return _moe_final(x, row(norm_ffn[l]), *moe, row(norm_final))

    def rows_of(pr, b, t):
        def tok_major(a, *s):
            c = a.shape[1]
            return jnp.moveaxis(jnp.moveaxis(a, 1, 0).reshape(c, b, t), 0, -1).reshape(b, t, *s)
        return (tok_major(pr["kat"], N_HEADS_A, HEAD_DIM), tok_major(pr["vat"], N_HEADS_A, HEAD_DIM),
                tok_major(pr["kwt"][:, :IDX_DIM], IDX_DIM), tok_major(pr["kbt"], N_HEADS_B, HEAD_DIM),
                tok_major(pr["vbt"], N_HEADS_B, HEAD_DIM),
                pr["u"].reshape(b, t, D_C)[:, t - (CONV_WIDTH - 1):])

    x = x_prompt.reshape(bsz * seq, D_MODEL)
    rows_p = []
    zeros_state = jnp.zeros((bsz, SUBLANES, D_C), F32)
    for l in range(depth):
        pr = _proj(x, row(norm_mix[l]), lw[l]["w_tok"], lw[l]["w_feat"], bsz, seq)
        rows_p.append(rows_of(pr, bsz, seq))
        oa = _dsa_prompt(pr, tiles_a, bsz, seq)
        ob = _diff_prompt(pr, tiles_b, lw[l]["lam"], lw[l]["g_col"], lw[l]["out_scale"], bsz, seq)
        x = _outproj(x, oa, ob, pr["u"], pr["gb"], lw[l]["cw"], lw[l]["w_out"], pr["u"], zeros_state,
                     seq, False)
        x = ffn_block(l, x)
    y_prompt = x.reshape(bsz, seq, D_MODEL)

    x = x_sample.reshape(n_s, D_MODEL)
    rows_s = []
    for l in range(depth):
        pr = _proj(x, row(norm_mix[l]), lw[l]["w_tok"], lw[l]["w_feat"], 1, n_s)
        rows_s.append(rows_of(pr, bd, n_new))
        r3 = lambda a: a.reshape(bd, n_new, a.shape[-1])
        new_rows = lambda a: _pad_rows(r3(a), LANES)
        new_rows_t = lambda a: _pad_rows(jnp.moveaxis(a[0].reshape(-1, bd, n_new), 0, -1), LANES).astype(BF16)
        qi_s = _pad_rows(r3(pr["qi"]).reshape(bd, n_new, IDX_HEADS, IDX_DIM).transpose(0, 2, 1, 3),
                         SUBLANES).reshape(bd, IDX_HEADS * SUBLANES, IDX_DIM)
        w_tok = jnp.moveaxis(pr["kwt"][0, IDX_DIM:IDX_DIM + IDX_HEADS].reshape(IDX_HEADS, bd, n_new), 0, -1)
        w_s = jnp.pad(_pad_rows(w_tok, SUBLANES) * IDX_HEADS ** -0.5, ((0, 0), (0, 0), (0, LANES - IDX_HEADS)))
        mb = _idx_sample(page_table, qi_s, w_s, new_rows(pr["kib"]), ca_i, l, past, n_new, N_HEADS_A)
        oa = _attn_sample(page_table, _block_diag_queries(r3(pr["qa"]), 1), mb, sbias_a,
                          new_rows(pr["kab"]), new_rows_t(pr["vatb"]), ca_k, ca_v, l, past)
        ob = _attn_sample(page_table, _block_diag_queries(r3(pr["qb"]), 2), None, sbias_b,
                          new_rows(pr["kbb"]), new_rows_t(pr["vbtb"]), cb_k, cb_v, l, past,
                          lam=lw[l]["lam"], g_row=lw[l]["g_row"], diff_scale=lw[l]["out_scale"])
        flat = lambda o: o[:, :n_new].reshape(n_s, o.shape[-1]).astype(BF16)
        st = state_conv[l]
        zero = jnp.zeros((bd, n_new - 1, D_C), F32)
        a = jnp.concatenate([st[:, 1:2], zero], axis=1).reshape(n_s, D_C)
        b = jnp.concatenate([st, zero[:, 1:]], axis=1).reshape(n_s, D_C)
        x = _outproj(x, flat(oa), flat(ob), pr["u"], pr["gb"], lw[l]["cw"], lw[l]["w_out"], a, b,
                     n_new, True)
        x = ffn_block(l, x)
    y_sample = x.reshape(bd, n_new, D_MODEL)

    stack = lambda rows: tuple(jnp.stack([r[i] for r in rows]) for i in range(6))
    return (y_prompt, y_sample) + stack(rows_p) + stack(rows_s)
```

```python
import functools
import math

import jax
import jax.numpy as jnp
from jax import lax
from jax.experimental import pallas as pl
from jax.experimental.pallas import tpu as pltpu

F32 = jnp.float32
BF16 = jnp.bfloat16
I32 = jnp.int32

D_MODEL = 1024
HEAD_DIM = 64
N_HEADS_A = 6
N_HEADS_B = 6
D_A = N_HEADS_A * HEAD_DIM
D_B = N_HEADS_B * HEAD_DIM
D_C = D_MODEL - D_A - D_B
DIFF_DIM = HEAD_DIM // 2
IDX_HEADS = 4
IDX_DIM = 64
TOPK_MAX = 256
CONV_WIDTH = 3
NUM_BUCKETS = 32
MAX_DISTANCE = 128
N_EXPERTS = 8
EPS = 1e-6

LANES = 128
SUBLANES = 8
VMEM_LIMIT = 56 * 1024 * 1024
LOG2E = 1.4426950408889634
NEG = -1e30
INT_MIN = -2**31
BIG_IDX = 2**30
I16 = jnp.int16
MIN16 = -2**15
BIG16 = 2**15 - 1
PACK16 = 16

ATT_TILE = 256
PAGES_PER_STEP = 16
N_PAIRS = D_A // LANES

R_QA, R_QI, R_QB, R_KA, R_KB, R_KI, R_HC, R_GB, R_GC, R_END = (
    0, 384, 640, 1024, 1408, 1792, 1920, 2176, 2432, 2688)
T_KA, T_VA, T_KB, T_VB, T_KW, T_END = 0, 384, 768, 1152, 1536, 1664


def _cparams(sem):
    return pltpu.CompilerParams(dimension_semantics=sem, vmem_limit_bytes=VMEM_LIMIT)


def _dot_t(a, b):
    return lax.dot_general(a, b, (((1,), (1,)), ((), ())), preferred_element_type=F32)


def _dot(a, b):
    return jnp.dot(a, b, preferred_element_type=F32)


def _rms(x, g):
    ms = jnp.mean(x * x, axis=-1, keepdims=True)
    return x * lax.rsqrt(ms + EPS) * g


def _proj_kernel(x_ref, g_ref, w_ref, wt_ref, qa_ref, qi_ref, qb_ref, kab_ref, kbb_ref, kib_ref,
                 u_ref, gb_ref, kat_ref, vat_ref, vatb_ref, kbt_ref, vbt_ref, vbtb_ref, kwt_ref):
    h = _rms(x_ref[...], g_ref[...]).astype(BF16)

    def mm(a, b):
        return _dot(h, w_ref[:, a:b])

    def mt(a, b):
        return _dot_t(wt_ref[a:b, :], h)

    qa_ref[...] = (mm(R_QA, R_QI) * (HEAD_DIM ** -0.5 * LOG2E)).astype(BF16)
    qi_ref[...] = (mm(R_QI, R_QB) * IDX_DIM ** -0.5).astype(BF16)
    qb_ref[...] = (mm(R_QB, R_KA) * (DIFF_DIM ** -0.5 * LOG2E)).astype(BF16)
    kab_ref[...] = mm(R_KA, R_KB).astype(BF16)
    kbb_ref[...] = mm(R_KB, R_KI).astype(BF16)
    kib_ref[...] = mm(R_KI, R_HC)[:, :IDX_DIM].astype(BF16)
    hc = mm(R_HC, R_GB)
    gb_ref[...] = mm(R_GB, R_GC)
    u_ref[...] = mm(R_GC, R_END) * hc
    kat_ref[...] = mt(T_KA, T_VA)
    va = mt(T_VA, T_KB)
    vat_ref[...] = va
    vatb_ref[...] = va.astype(BF16)
    kbt_ref[...] = mt(T_KB, T_VB)
    vb = mt(T_VB, T_KW)
    vbt_ref[...] = vb
    vbtb_ref[...] = vb.astype(BF16)
    kwt_ref[...] = mt(T_KW, T_END)


def _proj(x, g, w, wt, bsz, t):
    tm = min(ATT_TILE, t)
    assert t % tm == 0
    tpb = t // tm
    tok = dict(qa=(D_A, BF16), qi=(IDX_HEADS * IDX_DIM, BF16), qb=(D_B, BF16), kab=(D_A, BF16),
               kbb=(D_B, BF16), kib=(IDX_DIM, BF16), u=(D_C, F32), gb=(D_C, F32))
    feat = dict(kat=(D_A, F32), vat=(D_A, F32), vatb=(D_A, BF16), kbt=(D_B, F32), vbt=(D_B, F32),
                vbtb=(D_B, BF16), kwt=(LANES, F32))
    outs = pl.pallas_call(
        _proj_kernel,
        grid=(bsz * tpb,),
        in_specs=[pl.BlockSpec((tm, D_MODEL), lambda i: (i, 0)),
                  pl.BlockSpec((1, D_MODEL), lambda i: (0, 0)),
                  pl.BlockSpec((D_MODEL, R_END), lambda i: (0, 0)),
                  pl.BlockSpec((T_END, D_MODEL), lambda i: (0, 0))],
        out_specs=[pl.BlockSpec((tm, c), lambda i: (i, 0)) for c, _ in tok.values()]
        + [pl.BlockSpec((None, c, tm), lambda i: (i // tpb, 0, i % tpb)) for c, _ in feat.values()],
        out_shape=[jax.ShapeDtypeStruct((bsz * t, c), d) for c, d in tok.values()]
        + [jax.ShapeDtypeStruct((bsz, c, t), d) for c, d in feat.values()],
        compiler_params=_cparams(("parallel",)),
        name="proj",
    )(x, g, w, wt)
    return dict(zip(list(tok) + list(feat), outs))


def _sortable_key(score):
    score = jnp.where(score == 0.0, 0.0, score)
    bits = pltpu.bitcast(score, I32)
    return bits ^ ((bits >> 31) & 0x7FFFFFFF)


def _topk_search(count, transform, shape, topk, idx_bits):
    kf = float(topk)
    t = jnp.where(count(lambda blk, c: blk >= 0, None) >= kf, 0, INT_MIN).astype(I32)

    def value_bit(i, t):
        cand = t | lax.shift_left(jnp.int32(1), 30 - i)
        return jnp.where(count(lambda blk, c: blk >= c, cand) >= kf, cand, t)

    t = lax.fori_loop(0, 31, value_bit, t)
    transform(jnp.maximum(t, INT_MIN + 1))

    def index_bit(i, m):
        cand = m | lax.shift_left(jnp.int32(1), idx_bits - 1 - i)
        return jnp.where(count(lambda blk, c: blk < c, cand) < kf, cand, m)

    return lax.fori_loop(0, idx_bits, index_bit, jnp.zeros(shape, I32))


def _rewrite_keys(blk, t, idx):
    return jnp.where(blk > t, -1, jnp.where(blk == t, idx, BIG_IDX))


def _tree_sum(parts):
    while len(parts) > 1:
        parts = [a + b for a, b in zip(parts[::2], parts[1::2])] + parts[len(parts) & ~1:]
    return parts[0]


def _topk_search_16(hi_ref, lo_ref, n_ch, tk, topk, idx_bits):
    tq = hi_ref.shape[1]
    one, zero = jnp.int16(1), jnp.int16(0)
    kf = float(topk)
    row = lax.broadcasted_iota(I32, (tk, tq), 0)

    def chunk(kc):
        return pl.ds(pl.multiple_of(kc * tk, tk), tk)

    def count(ref, pred, cand):
        c16 = jnp.broadcast_to(cand, (PACK16, tq)).astype(I16)

        def body(i, acc):
            blks = [ref[chunk(2 * i + s), :] for s in range(2)]
            return acc + _tree_sum([jnp.where(pred(blk[g * PACK16:(g + 1) * PACK16], c16), one, zero)
                                    for blk in blks for g in range(tk // PACK16)])

        acc = lax.fori_loop(0, n_ch // 2, body, jnp.zeros((PACK16, tq), I16))
        return jnp.sum(acc.astype(F32), axis=0, keepdims=True)

    def largest_with(ref, need):
        ge = lambda c: count(ref, lambda b, c16: b >= c16, c)
        t = jnp.where(ge(jnp.zeros((1, tq), I32)) >= need, 0, MIN16).astype(I32)
        return lax.fori_loop(
            0, 15, lambda i, t: jnp.where(ge(t | lax.shift_left(jnp.int32(1), 14 - i)) >= need,
                                          t | lax.shift_left(jnp.int32(1), 14 - i), t), t)

    def full16(v):
        return jnp.broadcast_to(v, (tk, tq)).astype(I16)

    t_hi = jnp.maximum(largest_with(hi_ref, kf), MIN16 + 1)
    need = kf - count(hi_ref, lambda b, c16: b > c16, t_hi)
    thi_f = full16(t_hi)

    def keep_candidates(kc, c):
        lo_ref[chunk(kc), :] = jnp.where(hi_ref[chunk(kc), :] == thi_f, lo_ref[chunk(kc), :], jnp.int16(MIN16))
        return c

    lax.fori_loop(0, n_ch, keep_candidates, 0)
    tlo_f = full16(largest_with(lo_ref, need))
    minus1, big = jnp.int16(-1), jnp.int16(BIG16)

    def rewrite(kc, c):
        hi = hi_ref[chunk(kc), :]
        lo = lo_ref[chunk(kc), :]
        idx = (kc * tk + row).astype(I16)
        inner = jnp.where(lo > tlo_f, minus1, jnp.where(lo == tlo_f, idx, big))
        hi_ref[chunk(kc), :] = jnp.where(hi > thi_f, minus1, jnp.where(hi == thi_f, inner, big))
        return c

    lax.fori_loop(0, n_ch, rewrite, 0)

    def index_bit(i, m):
        cand = m | lax.shift_left(jnp.int32(1), idx_bits - 1 - i)
        return jnp.where(count(hi_ref, lambda b, c16: b < c16, cand) < kf, cand, m)

    return lax.fori_loop(0, idx_bits, index_bit, jnp.zeros((1, tq), I32))


N_TILE_VARIANTS = 4


def _tile_variant(kc, qi):
    return jnp.where(kc == qi, 0, jnp.where(kc == qi - 1, 1, jnp.where(kc > qi, 3, 2)))


def _softmax_step_t(lg_of, vt, m_ref, l_ref, acc_ref):
    alphas, probs = [], []
    for j in range(m_ref.shape[-1] // LANES):
        cols = slice(j * LANES, (j + 1) * LANES)
        lg = lg_of(cols)
        m_old = m_ref[:, cols]
        m_new = jnp.maximum(m_old, jnp.max(lg, axis=0, keepdims=True))
        alpha = jnp.exp2(m_old - m_new)
        p = jnp.exp2(lg - m_new)
        l_ref[:, cols] = alpha * l_ref[:, cols] + jnp.sum(p, axis=0, keepdims=True)
        m_ref[:, cols] = m_new
        alphas.append(alpha)
        probs.append(p.astype(BF16))
    acc_ref[...] = (jnp.concatenate(alphas, axis=1) * acc_ref[...]
                    + _dot(vt, jnp.concatenate(probs, axis=1)))


def _dsa_prompt_kernel(qi_ref, kwt_ref, kib_ref, qa_ref, kab_ref, vat_ref, tiles_ref, oa_ref,
                       hi_sc, lo_sc, mb_sc, lg_sc, qm_sc, m_sc, l_sc, acc_sc, *, topk, idx_bits):
    tq = ATT_TILE
    qi = pl.program_id(1)
    n_ch = qi + 1
    q0 = qi * tq
    row = lax.broadcasted_iota(I32, (tq, tq), 0)
    col = lax.broadcasted_iota(I32, (tq, tq), 1)

    w4 = kwt_ref[IDX_DIM:IDX_DIM + SUBLANES, :] * IDX_HEADS ** -0.5

    n_trips = (n_ch + 1) // 2

    def score_chunks(i, c):
        for s in range(2):
            kc = 2 * i + s
            k0 = pl.multiple_of(kc * tq, tq)
            kch = kib_ref[pl.ds(pl.multiple_of(jnp.minimum(kc, pl.num_programs(1) - 1) * tq, tq), tq), :]
            sc = jnp.zeros((tq, tq), F32)
            for h in range(IDX_HEADS):
                sc = sc + w4[h:h + 1, :] * jnp.maximum(_dot_t(kch, qi_ref[:, h * IDX_DIM:(h + 1) * IDX_DIM]), 0.0)
            valid = (k0 + row) <= (q0 + col)
            key = jnp.where(valid, _sortable_key(sc), INT_MIN)
            hi_sc[pl.ds(k0, tq), :] = (key >> 16).astype(I16)
            lo_sc[pl.ds(k0, tq), :] = ((key & 0xFFFF) + MIN16).astype(I16)
        return c

    lax.fori_loop(0, n_trips, score_chunks, 0)

    thr = _topk_search_16(hi_sc, lo_sc, 2 * n_trips, tq, topk, idx_bits)

    lane = lax.broadcasted_iota(I32, (tq, LANES), 1)
    for p in range(N_PAIRS):
        qp = qa_ref[:, p * LANES:(p + 1) * LANES]
        qm_sc[p, 0:tq, :] = jnp.where(lane < HEAD_DIM, qp, jnp.zeros_like(qp))
        qm_sc[p, tq:2 * tq, :] = jnp.where(lane >= HEAD_DIM, qp, jnp.zeros_like(qp))
    m_sc[...] = jnp.full(m_sc.shape, NEG, F32)
    l_sc[...] = jnp.zeros(l_sc.shape, F32)
    acc_sc[...] = jnp.zeros(acc_sc.shape, F32)

    def logits_into(slot, kc):
        k0 = pl.multiple_of(kc * tq, tq)
        for p in range(N_PAIRS):
            kp = kab_ref[pl.ds(k0, tq), p * LANES:(p + 1) * LANES]
            lg_sc[slot, :, 2 * p * tq:2 * (p + 1) * tq] = _dot_t(kp, qm_sc[p])

    last = pl.num_programs(1) - 1
    logits_into(0, 0)
    logits_into(1, jnp.minimum(1, last))

    def attend(i, c):
        for s in range(2):
            kc = 2 * i + s
            k0 = pl.multiple_of(jnp.minimum(kc, last) * tq, tq)
            var = _tile_variant(kc, qi)
            mb_sc[s] = jnp.where(hi_sc[pl.ds(k0, tq), :].astype(I32) <= thr, 0.0, NEG)
            for h in range(N_HEADS_A):
                p = h // 2
                vt = vat_ref[p * LANES:(p + 1) * LANES, pl.ds(k0, tq)]

                def lg_of(cols, h=h):
                    shifted = slice(h * tq + cols.start, h * tq + cols.stop)
                    return lg_sc[s, :, shifted] + tiles_ref[var, h, :, cols] + mb_sc[s, :, cols]

                _softmax_step_t(lg_of, vt, m_sc.at[h], l_sc.at[h], acc_sc.at[h])
            logits_into(s, jnp.minimum(kc + 2, last))
        return c

    lax.fori_loop(0, (n_ch + 1) // 2, attend, 0)

    srow = lax.broadcasted_iota(I32, (LANES, tq), 0)
    for p in range(N_PAIRS):
        o_lo = acc_sc[2 * p] / l_sc[2 * p]
        o_hi = acc_sc[2 * p + 1] / l_sc[2 * p + 1]
        o_t = jnp.where(srow < HEAD_DIM, o_lo, o_hi)
        oa_ref[:, p * LANES:(p + 1) * LANES] = o_t.T.astype(BF16)


def _dsa_prompt(pr, tiles, bsz, t):
    tq = ATT_TILE
    assert t % tq == 0
    topk = min(TOPK_MAX, t // 4)
    idx_bits = max(1, (t - 1).bit_length())
    r3 = lambda a: a.reshape(bsz, t, a.shape[-1])
    qtile = lambda c: pl.BlockSpec((None, tq, c), lambda b, i: (b, i, 0))
    full = lambda c: pl.BlockSpec((None, t, c), lambda b, i: (b, 0, 0))
    out = pl.pallas_call(
        functools.partial(_dsa_prompt_kernel, topk=topk, idx_bits=idx_bits),
        grid=(bsz, t // tq),
        in_specs=[qtile(IDX_HEADS * IDX_DIM),
                  pl.BlockSpec((None, LANES, tq), lambda b, i: (b, 0, i)),
                  full(IDX_DIM), qtile(D_A), full(D_A),
                  pl.BlockSpec((None, D_A, t), lambda b, i: (b, 0, 0)),
                  pl.BlockSpec((N_TILE_VARIANTS, N_HEADS_A, tq, tq), lambda b, i: (0, 0, 0, 0))],
        out_specs=qtile(D_A),
        out_shape=jax.ShapeDtypeStruct((bsz, t, D_A), BF16),
        scratch_shapes=[pltpu.VMEM((t + tq, tq), I16), pltpu.VMEM((t + tq, tq), I16),
                        pltpu.VMEM((2, tq, tq), F32),
                        pltpu.VMEM((2, tq, N_HEADS_A * tq), F32),
                        pltpu.VMEM((N_PAIRS, 2 * tq, LANES), BF16),
                        pltpu.VMEM((N_HEADS_A, 1, tq), F32), pltpu.VMEM((N_HEADS_A, 1, tq), F32),
                        pltpu.VMEM((N_HEADS_A, LANES, tq), F32)],
        compiler_params=_cparams(("parallel", "arbitrary")),
        name="dsa_prompt",
    )(r3(pr["qi"]), pr["kwt"], r3(pr["kib"]), r3(pr["qa"]), r3(pr["kab"]), pr["vatb"], tiles)
    return out.reshape(bsz * t, D_A)


def _diff_prompt_kernel(lam_ref, qb_ref, kbb_ref, vbt_ref, tiles_ref, g_ref, ob_ref,
                        lg_sc, qm_sc, m_sc, l_sc, acc_sc, *, out_scale):
    tq = ATT_TILE
    qi = pl.program_id(1)
    lane = lax.broadcasted_iota(I32, (tq, LANES), 1)
    for p in range(N_PAIRS):
        qp = qb_ref[:, p * LANES:(p + 1) * LANES]
        for v in range(4):
            inside = (lane >= v * DIFF_DIM) & (lane < (v + 1) * DIFF_DIM)
            qm_sc[p, v * tq:(v + 1) * tq, :] = jnp.where(inside, qp, jnp.zeros_like(qp))
    m_sc[...] = jnp.full(m_sc.shape, NEG, F32)
    l_sc[...] = jnp.zeros(l_sc.shape, F32)
    acc_sc[...] = jnp.zeros(acc_sc.shape, F32)

    n_ch = qi + 1

    def logits_into(slot, kc):
        k0 = pl.multiple_of(kc * tq, tq)
        for p in range(N_PAIRS):
            kp = kbb_ref[pl.ds(k0, tq), p * LANES:(p + 1) * LANES]
            for hh in range(2):
                u0 = (4 * p + 2 * hh) * tq
                lg_sc[slot, :, u0:u0 + 2 * tq] = _dot_t(kp, qm_sc[p, 2 * hh * tq:2 * (hh + 1) * tq, :])

    last = pl.num_programs(1) - 1
    logits_into(0, 0)
    logits_into(1, jnp.minimum(1, last))

    def attend(i, c):
        for s in range(2):
            kc = 2 * i + s
            k0 = pl.multiple_of(jnp.minimum(kc, last) * tq, tq)
            var = _tile_variant(kc, qi)
            for p in range(N_PAIRS):
                vt = vbt_ref[p * LANES:(p + 1) * LANES, pl.ds(k0, tq)]
                for v in range(4):
                    def lg_of(cols, u=4 * p + v, h=2 * p + v // 2):
                        shifted = slice(u * tq + cols.start, u * tq + cols.stop)
                        return lg_sc[s, :, shifted] + tiles_ref[var, h, :, cols]

                    _softmax_step_t(lg_of, vt, m_sc.at[p, v], l_sc.at[p, v], acc_sc.at[p, v])
            logits_into(s, jnp.minimum(kc + 2, last))
        return c

    lax.fori_loop(0, (n_ch + 1) // 2, attend, 0)

    lam = lam_ref[0]
    lo = lax.broadcasted_iota(I32, (LANES, tq), 0) < HEAD_DIM
    for p in range(N_PAIRS):
        def head(v1, v2):
            return acc_sc[p, v1] / l_sc[p, v1] - lam * (acc_sc[p, v2] / l_sc[p, v2])

        o = jnp.where(lo, head(0, 1), head(2, 3))
        sq = o * o
        ms_lo = jnp.sum(jnp.where(lo, sq, 0.0), axis=0, keepdims=True) * (1.0 / HEAD_DIM)
        ms_hi = jnp.sum(jnp.where(lo, 0.0, sq), axis=0, keepdims=True) * (1.0 / HEAD_DIM)
        r = jnp.where(lo, lax.rsqrt(ms_lo + EPS), lax.rsqrt(ms_hi + EPS))
        y = o * r * g_ref[...] * out_scale
        ob_ref[:, p * LANES:(p + 1) * LANES] = y.T.astype(BF16)


def _diff_prompt(pr, tiles, lam, g_col, out_scale, bsz, t):
    tq = ATT_TILE
    r3 = lambda a: a.reshape(bsz, t, a.shape[-1])
    qtile = lambda c: pl.BlockSpec((None, tq, c), lambda b, i: (b, i, 0))
    full = lambda c: pl.BlockSpec((None, t, c), lambda b, i: (b, 0, 0))
    out = pl.pallas_call(
        functools.partial(_diff_prompt_kernel, out_scale=out_scale),
        grid=(bsz, t // tq),
        in_specs=[pl.BlockSpec(memory_space=pltpu.SMEM), qtile(D_B), full(D_B),
                  pl.BlockSpec((None, D_B, t), lambda b, i: (b, 0, 0)),
                  pl.BlockSpec((N_TILE_VARIANTS, N_HEADS_B, tq, tq), lambda b, i: (0, 0, 0, 0)),
                  pl.BlockSpec((LANES, 1), lambda b, i: (0, 0))],
        out_specs=qtile(D_B),
        out_shape=jax.ShapeDtypeStruct((bsz, t, D_B), BF16),
        scratch_shapes=[pltpu.VMEM((2, tq, 2 * N_HEADS_B * tq), F32),
                        pltpu.VMEM((N_PAIRS, 4 * tq, LANES), BF16),
                        pltpu.VMEM((N_PAIRS, 4, 1, tq), F32), pltpu.VMEM((N_PAIRS, 4, 1, tq), F32),
                        pltpu.VMEM((N_PAIRS, 4, LANES, tq), F32)],
        compiler_params=_cparams(("parallel", "arbitrary")),
        name="diff_prompt",
    )(lam, r3(pr["qb"]), r3(pr["kbb"]), pr["vbtb"], tiles, g_col)
    return out.reshape(bsz * t, D_B)


def _gather_chunk(page_refs):
    return jnp.concatenate([r[...] for r in page_refs], axis=1).astype(BF16)


def _idx_sample_kernel(pt_ref, qi_ref, w_ref, kin_ref, *rest, past, topk, idx_bits, n_rep):
    del pt_ref
    pages = rest[:PAGES_PER_STEP]
    mb_ref, s_sc = rest[PAGES_PER_STEP:]
    g = pl.program_id(1)
    ck = PAGES_PER_STEP * LANES
    w4 = w_ref[...]

    def scores(s_all):
        sc = jnp.zeros((SUBLANES, s_all.shape[1]), F32)
        for h in range(IDX_HEADS):
            sc = sc + w4[:, h:h + 1] * jnp.maximum(s_all[h * SUBLANES:(h + 1) * SUBLANES], 0.0)
        return _sortable_key(sc)

    s_sc[:, pl.ds(pl.multiple_of(g * ck, ck), ck)] = scores(_dot(qi_ref[...], _gather_chunk(pages)))

    @pl.when(g == pl.num_programs(1) - 1)
    def _():
        row = lax.broadcasted_iota(I32, (SUBLANES, LANES), 0)
        col = lax.broadcasted_iota(I32, (SUBLANES, LANES), 1)
        new = scores(_dot_t(qi_ref[...], kin_ref[...]))
        s_sc[:, past:past + LANES] = jnp.where(col <= row, new, INT_MIN)
        n_groups = past // LANES + 1

        def group(i):
            return slice(i * LANES, (i + 1) * LANES)

        def count(pred, cand):
            cb = None if cand is None else jnp.broadcast_to(cand, (SUBLANES, LANES))
            parts = [jnp.where(pred(s_sc[:, group(i)], cb), 1, 0) for i in range(n_groups)]
            while len(parts) > 1:
                parts = [a + b for a, b in zip(parts[::2], parts[1::2])] + parts[len(parts) & ~1:]
            return jnp.sum(parts[0].astype(F32), axis=1, keepdims=True)

        def transform(t):
            for i in range(n_groups):
                s_sc[:, group(i)] = _rewrite_keys(s_sc[:, group(i)], t, i * LANES + col)

        m = _topk_search(count, transform, (SUBLANES, 1), topk, idx_bits)
        mb = jnp.where(s_sc[...] <= m, 0.0, NEG)
        mb_ref[...] = jnp.concatenate([mb] * n_rep, axis=0)


def _page_specs(width, layer, n):
    return [pl.BlockSpec((None, None, width, LANES),
                         functools.partial(lambda b, g, pt, j: (layer, pt[b, g * PAGES_PER_STEP + j], 0, 0), j=j))
            for j in range(n)]


def _idx_sample(page_table, qi_s, w_s, kin_s, cache_kidx_t, layer, past, n_new, n_rep):
    bd = qi_s.shape[0]
    lp = past + LANES
    n_steps = past // (PAGES_PER_STEP * LANES)
    topk = min(TOPK_MAX, (past + n_new) // 4)
    idx_bits = (past + n_new - 1).bit_length()
    per_b = lambda r, c: pl.BlockSpec((None, r, c), lambda b, g, pt: (b, 0, 0))
    return pl.pallas_call(
        functools.partial(_idx_sample_kernel, past=past, topk=topk, idx_bits=idx_bits, n_rep=n_rep),
        grid_spec=pltpu.PrefetchScalarGridSpec(
            num_scalar_prefetch=1, grid=(bd, n_steps),
            in_specs=[per_b(IDX_HEADS * SUBLANES, IDX_DIM), per_b(SUBLANES, LANES), per_b(LANES, IDX_DIM)]
            + _page_specs(IDX_DIM, layer, PAGES_PER_STEP),
            out_specs=per_b(n_rep * SUBLANES, lp),
            scratch_shapes=[pltpu.VMEM((SUBLANES, lp), I32)]),
        out_shape=jax.ShapeDtypeStruct((bd, n_rep * SUBLANES, lp), F32),
        compiler_params=_cparams(("parallel", "arbitrary")),
        name="idx_sample",
    )(page_table, qi_s, w_s, kin_s, *([cache_kidx_t] * PAGES_PER_STEP))


def _attn_sample_kernel(pt_ref, *refs, past, use_mask, diff_scale):
    del pt_ref
    refs = list(refs)
    lam_ref = refs.pop(0) if diff_scale is not None else None
    q_ref = refs.pop(0)
    mb_ref = refs.pop(0) if use_mask else None
    bias_ref, kn_ref, vn_ref = refs[:3]
    refs = refs[3:]
    g_ref = refs.pop(0) if diff_scale is not None else None
    kpages = refs[:PAGES_PER_STEP]
    vpages = refs[PAGES_PER_STEP:2 * PAGES_PER_STEP]
    o_ref, m_sc, l_sc, acc_sc = refs[2 * PAGES_PER_STEP:]
    g = pl.program_id(1)
    ck = PAGES_PER_STEP * LANES

    @pl.when(g == 0)
    def _():
        m_sc[...] = jnp.full(m_sc.shape, NEG, F32)
        l_sc[...] = jnp.zeros(l_sc.shape, F32)
        acc_sc[...] = jnp.zeros(acc_sc.shape, F32)

    def attend(lg, c0, width, pv):
        lg = lg + bias_ref[:, pl.ds(c0, width)]
        if use_mask:
            lg = lg + mb_ref[:, pl.ds(c0, width)]
        m_old = m_sc[...]
        m_new = jnp.maximum(m_old, jnp.max(lg, axis=1, keepdims=True))
        alpha = jnp.exp2(m_old - m_new)
        p = jnp.exp2(lg - m_new)
        l_sc[...] = alpha * l_sc[...] + jnp.sum(p, axis=1, keepdims=True)
        acc_sc[...] = alpha * acc_sc[...] + pv(p.astype(BF16))
        m_sc[...] = m_new

    q = q_ref[...]
    vt = _gather_chunk(vpages)
    attend(_dot(q, _gather_chunk(kpages)), pl.multiple_of(g * ck, ck), ck, lambda p: _dot_t(p, vt))

    @pl.when(g == pl.num_programs(1) - 1)
    def _():
        attend(_dot_t(q, kn_ref[...]), past, LANES, lambda p: _dot(p, vn_ref[...]))
        head_of_col = lax.broadcasted_iota(I32, (SUBLANES, D_A), 1) // HEAD_DIM
        n_comp = q.shape[0] // (N_HEADS_A * SUBLANES)

        def gather_heads(comp):
            out = jnp.zeros((SUBLANES, D_A), F32)
            for h in range(N_HEADS_A):
                r0 = (h * n_comp + comp) * SUBLANES
                o_h = acc_sc[r0:r0 + SUBLANES, :] / l_sc[r0:r0 + SUBLANES, :]
                out = jnp.where(head_of_col == h, o_h, out)
            return out

        if diff_scale is None:
            o_ref[...] = gather_heads(0)
        else:
            o = gather_heads(0) - lam_ref[0] * gather_heads(1)
            sq = o * o
            r = jnp.zeros((SUBLANES, D_A), F32)
            for h in range(N_HEADS_B):
                ms = jnp.sum(jnp.where(head_of_col == h, sq, 0.0), axis=1, keepdims=True) * (1.0 / HEAD_DIM)
                r = jnp.where(head_of_col == h, lax.rsqrt(ms + EPS), r)
            o_ref[...] = o * r * g_ref[...] * diff_scale


def _attn_sample(page_table, q_bd, mb, bias_s, kn_s, vn_s, cache_kt, cache_vt, layer, past,
                 lam=None, g_row=None, diff_scale=None):
    bd, rows, _ = q_bd.shape
    lp = past + LANES
    n_steps = past // (PAGES_PER_STEP * LANES)
    use_mask = mb is not None
    per_b = lambda *s: pl.BlockSpec((None,) + s, lambda b, g, pt: (b,) + (0,) * len(s))
    const = lambda *s: pl.BlockSpec(s, lambda b, g, pt: (0,) * len(s))
    args, specs = [], []
    if diff_scale is not None:
        args.append(lam)
        specs.append(pl.BlockSpec(memory_space=pltpu.SMEM))
    args.append(q_bd)
    specs.append(per_b(rows, D_A))
    if use_mask:
        args.append(mb)
        specs.append(per_b(rows, lp))
    args += [bias_s, kn_s, vn_s]
    specs += [const(rows, lp), per_b(LANES, D_A), per_b(LANES, D_A)]
    if diff_scale is not None:
        args.append(g_row)
        specs.append(const(1, D_A))
    args += [cache_kt] * PAGES_PER_STEP + [cache_vt] * PAGES_PER_STEP
    specs += _page_specs(D_A, layer, PAGES_PER_STEP) + _page_specs(D_A, layer, PAGES_PER_STEP)
    return pl.pallas_call(
        functools.partial(_attn_sample_kernel, past=past, use_mask=use_mask, diff_scale=diff_scale),
        grid_spec=pltpu.PrefetchScalarGridSpec(
            num_scalar_prefetch=1, grid=(bd, n_steps), in_specs=specs,
            out_specs=per_b(SUBLANES, D_A),
            scratch_shapes=[pltpu.VMEM((rows, 1), F32), pltpu.VMEM((rows, 1), F32),
                            pltpu.VMEM((rows, D_A), F32)]),
        out_shape=jax.ShapeDtypeStruct((bd, SUBLANES, D_A), F32),
        compiler_params=_cparams(("parallel", "arbitrary")),
        name="attn_sample_diff" if diff_scale is not None else "attn_sample_dsa",
    )(page_table, *args)


def _outproj_kernel(x_ref, oa_ref, ob_ref, u_ref, a_ref, b_ref, gb_ref, cw_ref, wo_ref, o_ref,
                    *, seq, per_row_state):
    tm = x_ref.shape[0]
    u = u_ref[...]
    row = lax.broadcasted_iota(I32, (tm, D_C), 0)
    um1 = pltpu.roll(u, 1, axis=0)
    um2 = pltpu.roll(u, 2, axis=0)
    if per_row_state:
        t = row % seq
        um1 = jnp.where(t == 0, a_ref[...], um1)
        um2 = jnp.where(t < 2, b_ref[...], um2)
    else:
        first = (pl.program_id(0) % (seq // tm)) == 0
        prev = jnp.where(first, b_ref[...], a_ref[...])
        p6 = prev[SUBLANES - 2:SUBLANES - 1, :]
        p7 = prev[SUBLANES - 1:SUBLANES, :]
        um1 = jnp.where(row == 0, p7, um1)
        um2 = jnp.where(row == 0, p6, jnp.where(row == 1, p7, um2))
    y = cw_ref[0:1, :] * um2 + cw_ref[1:2, :] * um1 + cw_ref[2:3, :] * u
    oc = (gb_ref[...] * y).astype(BF16)
    mixed = (_dot(oa_ref[...], wo_ref[0:D_A, :]) + _dot(ob_ref[...], wo_ref[D_A:D_A + D_B, :])
             + _dot(oc, wo_ref[D_A + D_B:D_MODEL, :]))
    o_ref[...] = x_ref[...] + mixed


def _outproj(x, oa, ob, u, gb, cw, wo, a, b, seq, per_row_state):
    n = x.shape[0]
    tm = min(512, n)
    assert n % tm == 0
    rowt = lambda c: pl.BlockSpec((tm, c), lambda i: (i, 0))
    if per_row_state:
        a_spec, b_spec = rowt(D_C), rowt(D_C)
    else:
        assert seq % tm == 0
        a_spec = pl.BlockSpec((SUBLANES, D_C), lambda i: (jnp.maximum(i * (tm // SUBLANES) - 1, 0), 0))
        b_spec = pl.BlockSpec((None, SUBLANES, D_C), lambda i: (i // (seq // tm), 0, 0))
    return pl.pallas_call(
        functools.partial(_outproj_kernel, seq=seq, per_row_state=per_row_state),
        grid=(n // tm,),
        in_specs=[rowt(D_MODEL), rowt(D_A), rowt(D_B), rowt(D_C), a_spec, b_spec, rowt(D_C),
                  pl.BlockSpec((SUBLANES, D_C), lambda i: (0, 0)),
                  pl.BlockSpec((D_MODEL, D_MODEL), lambda i: (0, 0))],
        out_specs=rowt(D_MODEL),
        out_shape=jax.ShapeDtypeStruct((n, D_MODEL), F32),
        compiler_params=_cparams(("parallel",)),
        name="outproj",
    )(x, oa, ob, u, a, b, gb, cw, wo)


def _silu(a):
    return a / (1.0 + jnp.exp(-a))


def _ffn_kernel(x_ref, g_ref, w1_ref, w3_ref, w2_ref, o_ref, *, n_chunks):
    x = x_ref[...]
    h = _rms(x, g_ref[...]).astype(BF16)
    tf = w1_ref.shape[1] // n_chunks
    out = x
    for c in range(n_chunks):
        a1 = _dot(h, w1_ref[:, c * tf:(c + 1) * tf])
        a3 = _dot(h, w3_ref[:, c * tf:(c + 1) * tf])
        out = out + _dot((_silu(a1) * a3).astype(BF16), w2_ref[c * tf:(c + 1) * tf, :])
    o_ref[...] = out


def _ffn(x, g, w1, w3, w2):
    n = x.shape[0]
    f = w1.shape[1]
    tm = min(256, n)
    n_chunks = 2 if f % (2 * LANES) == 0 else 1
    const = lambda s: pl.BlockSpec(s, lambda i: (0, 0), pipeline_mode=pl.Buffered(1))
    return pl.pallas_call(
        functools.partial(_ffn_kernel, n_chunks=n_chunks),
        grid=(n // tm,),
        in_specs=[pl.BlockSpec((tm, D_MODEL), lambda i: (i, 0)), pl.BlockSpec((1, D_MODEL), lambda i: (0, 0)),
                  const((D_MODEL, f)), const((D_MODEL, f)), const((f, D_MODEL))],
        out_specs=pl.BlockSpec((tm, D_MODEL), lambda i: (i, 0)),
        out_shape=jax.ShapeDtypeStruct((n, D_MODEL), F32),
        compiler_params=_cparams(("parallel",)),
        name="ffn",
    )(x, g, w1, w3, w2)


def _moe_kernel(x_ref, g_ref, wr_ref, br_ref, w1_ref, w3_ref, w2_ref, gf_ref, o_ref,
                h_sc, gate_sc, rank_sc, rank_t_sc, xg_sc, og_sc, acc_sc, *, cs):
    e = pl.program_id(1)
    f = pl.program_id(2)
    tm = x_ref.shape[0]
    lane = lax.broadcasted_iota(I32, (tm, LANES), 1)

    @pl.when((e == 0) & (f == 0))
    def _():
        h = _rms(x_ref[...], g_ref[...])
        h_sc[...] = h.astype(BF16)
        logits = jnp.dot(h, wr_ref[...], preferred_element_type=F32,
                         precision=lax.Precision.HIGHEST) + br_ref[...]
        lane_f = lane.astype(F32)
        top1 = jnp.max(logits, axis=1, keepdims=True)
        i1 = jnp.min(jnp.where(logits == top1, lane_f, float(LANES)), axis=1, keepdims=True)
        rest = jnp.where(lane_f == i1, NEG, logits)
        top2 = jnp.max(rest, axis=1, keepdims=True)
        i2 = jnp.min(jnp.where(rest == top2, lane_f, float(LANES)), axis=1, keepdims=True)
        e2 = jnp.exp(top2 - top1)
        den = 1.0 + e2
        gate_sc[...] = jnp.where(lane_f == i1, 1.0 / den, 0.0) + jnp.where(lane_f == i2, e2 / den, 0.0)
        sel = (lane_f == i1) | (lane_f == i2)
        sel_b = jnp.where(sel, 1.0, 0.0).astype(BF16)
        rb = min(ATT_TILE, tm)
        for r0 in range(0, tm, rb):
            earlier = (lax.broadcasted_iota(I32, (rb, tm), 1)
                       < lax.broadcasted_iota(I32, (rb, tm), 0) + r0)
            rank_sc[r0:r0 + rb, :] = jnp.where(sel[r0:r0 + rb], _dot(jnp.where(earlier, 1.0, 0.0).astype(BF16), sel_b), -1.0)
        rank_t_sc[...] = rank_sc[...].T[0:SUBLANES, :]
        acc_sc[...] = jnp.zeros(acc_sc.shape, F32)

    rank = rank_sc[...]
    n_e = jnp.sum(jnp.where((lane == e) & (rank >= 0.0), 1.0, 0.0)).astype(I32)
    n_c = (n_e + (cs - 1)) // cs

    def slots(c):
        return pl.ds(pl.multiple_of(c * cs, cs), cs)

    @pl.when(f == 0)
    def _():
        rank_row = rank_t_sc[pl.ds(e, 1), :]
        slot = lax.broadcasted_iota(I32, (cs, tm), 0).astype(F32)

        def gather(c, carry):
            pick = jnp.where(rank_row == slot + (c * cs).astype(F32), 1.0, 0.0).astype(BF16)
            xg_sc[slots(c), :] = _dot(pick, h_sc[...]).astype(BF16)
            og_sc[slots(c), :] = jnp.zeros((cs, D_MODEL), F32)
            return carry

        lax.fori_loop(0, n_c, gather, 0)

    def expert_chunk(c, carry):
        xg = xg_sc[slots(c), :]
        act = (_silu(_dot(xg, w1_ref[...])) * _dot(xg, w3_ref[...])).astype(BF16)
        og_sc[slots(c), :] += _dot(act, w2_ref[...])
        return carry

    lax.fori_loop(0, n_c, expert_chunk, 0)

    @pl.when(f == pl.num_programs(2) - 1)
    def _():
        rank_col = jnp.sum(jnp.where(lane == e, rank, 0.0), axis=1, keepdims=True)
        gate_col = jnp.sum(jnp.where(lane == e, gate_sc[...], 0.0), axis=1, keepdims=True)
        slot = lax.broadcasted_iota(I32, (tm, cs), 1).astype(F32)

        def scatter(c, carry):
            place = jnp.where(rank_col == slot + (c * cs).astype(F32), 1.0, 0.0).astype(BF16)
            acc_sc[...] += gate_col * _dot(place, og_sc[slots(c), :].astype(BF16))
            return carry

        lax.fori_loop(0, n_c, scatter, 0)

    @pl.when((e == pl.num_programs(1) - 1) & (f == pl.num_programs(2) - 1))
    def _():
        o_ref[...] = _rms(x_ref[...] + acc_sc[...], gf_ref[...])


def _moe_final(x, g, wr, br, w1, w3, w2, g_final):
    n = x.shape[0]
    n_exp, _, f = w1.shape
    tm = min(1024, n)
    cs = min(3 * LANES, tm)
    cap = pl.cdiv(tm, cs) * cs
    n_chunks = 2 if f % (2 * LANES) == 0 else 1
    tf = f // n_chunks
    tok = lambda c: pl.BlockSpec((tm, c), lambda i, e, j: (i, 0))
    const = lambda r, c: pl.BlockSpec((r, c), lambda i, e, j: (0, 0))
    return pl.pallas_call(
        functools.partial(_moe_kernel, cs=cs),
        grid=(n // tm, n_exp, n_chunks),
        in_specs=[tok(D_MODEL), const(1, D_MODEL), const(D_MODEL, LANES), const(1, LANES),
                  pl.BlockSpec((None, D_MODEL, tf), lambda i, e, j: (e, 0, j)),
                  pl.BlockSpec((None, D_MODEL, tf), lambda i, e, j: (e, 0, j)),
                  pl.BlockSpec((None, tf, D_MODEL), lambda i, e, j: (e, j, 0)),
                  const(1, D_MODEL)],
        out_specs=tok(D_MODEL),
        out_shape=jax.ShapeDtypeStruct((n, D_MODEL), F32),
        scratch_shapes=[pltpu.VMEM((tm, D_MODEL), BF16), pltpu.VMEM((tm, LANES), F32),
                        pltpu.VMEM((tm, LANES), F32), pltpu.VMEM((SUBLANES, tm), F32),
                        pltpu.VMEM((cap, D_MODEL), BF16), pltpu.VMEM((cap, D_MODEL), F32),
                        pltpu.VMEM((tm, D_MODEL), F32)],
        compiler_params=_cparams(("parallel", "arbitrary", "arbitrary")),
        name="moe_final",
    )(x, g, wr, br, w1, w3, w2, g_final)


def _t5_bucket(n):
    max_exact = NUM_BUCKETS // 2
    nf = jnp.maximum(n, max_exact).astype(F32)
    large = max_exact + (jnp.log(nf / max_exact) / math.log(MAX_DISTANCE / max_exact)
                         * (NUM_BUCKETS - max_exact)).astype(I32)
    return jnp.where(n < max_exact, n, jnp.minimum(large, NUM_BUCKETS - 1))


def _bias_of_distance(rel_bias, d):
    buckets = _t5_bucket(jnp.arange(MAX_DISTANCE, dtype=I32))
    tab = (rel_bias * LOG2E).T.reshape((rel_bias.shape[1],) + (1,) * d.ndim + (NUM_BUCKETS,))
    out = jnp.broadcast_to(tab[..., 0], (rel_bias.shape[1],) + d.shape)
    for b in range(1, NUM_BUCKETS):
        out = jnp.where(d >= jnp.sum(buckets < b), tab[..., b], out)
    return jnp.where(d < 0, NEG, out)


def _prompt_tiles(rel_bias):
    i = jnp.arange(ATT_TILE, dtype=I32)
    d = i[None, :] - i[:, None]
    tiles = jnp.stack([_bias_of_distance(rel_bias, d + v * ATT_TILE) for v in range(3)]
                      + [_bias_of_distance(rel_bias, d - ATT_TILE)])
    return tiles[:, :N_HEADS_A], tiles[:, N_HEADS_A:]


def _sample_bias(rel_bias, past, n_new):
    i = jnp.arange(SUBLANES, dtype=I32)[:, None]
    k = jnp.arange(past + LANES, dtype=I32)[None, :]
    b = _bias_of_distance(rel_bias, past + i - k)
    b = jnp.where(k < past + n_new, b, NEG)
    return b[:N_HEADS_A], b[N_HEADS_A:]


def _pad_rows(a, rows):
    pad = [(0, 0)] * a.ndim
    pad[-2] = (0, rows - a.shape[-2])
    return jnp.pad(a, pad)


def _block_diag_queries(q, n_comp):
    bd = q.shape[0]
    qp = _pad_rows(q, SUBLANES)[:, None]
    width = HEAD_DIM // n_comp
    owner = jnp.arange(D_A) // width
    blocks = jnp.arange(N_HEADS_A * n_comp)
    keep = (owner[None, :] == blocks[:, None])[None, :, None, :]
    return jnp.where(keep, qp, jnp.zeros_like(qp)).reshape(bd, N_HEADS_A * n_comp * SUBLANES, D_A)


def _layer_weights(l, w_in, w_out, conv_w, subln, lam_q1, lam_k1, lam_q2, lam_k2):
    w = w_in[l]
    qa, ka, va, qi, kw, qb, kb, vb, hc, gb, gc = (
        w[:, a:b] for a, b in ((0, 384), (384, 768), (768, 1152), (1152, 1408), (1408, 1476),
                               (1476, 1860), (1860, 2244), (2244, 2628), (2628, 2884), (2884, 3140),
                               (3140, 3396)))
    kw = jnp.pad(kw, ((0, 0), (0, LANES - kw.shape[1])))
    w_tok = jnp.concatenate([qa, qi, qb, ka, kb, kw, hc, gb, gc], axis=1).astype(BF16)
    w_feat = jnp.concatenate([ka, va, kb, vb, kw], axis=1).T.astype(BF16)
    lam_init = 0.8 - 0.6 * math.exp(-0.3 * l)
    lam = (jnp.exp(jnp.sum(lam_q1[l] * lam_k1[l])) - jnp.exp(jnp.sum(lam_q2[l] * lam_k2[l])) + lam_init)
    return dict(w_tok=w_tok, w_feat=w_feat, w_out=w_out[l].astype(BF16), cw=_pad_rows(conv_w[l], SUBLANES),
                g_col=jnp.tile(subln[l], 2).reshape(LANES, 1), g_row=jnp.tile(subln[l], N_HEADS_B).reshape(1, D_B),
                lam=lam.reshape(1).astype(F32), out_scale=1.0 - lam_init)


def kernel(x_prompt, x_sample, cache_a_k, cache_a_v, cache_a_kidx, cache_b_k, cache_b_v, state_conv,
           page_table, w_in, w_out, norm_mix, norm_ffn, norm_final, rel_bias, lam_q1, lam_k1, lam_q2,
           lam_k2, subln, conv_w, ffn_w1, ffn_w3, ffn_w2, moe_router, moe_router_b, moe_w1, moe_w3,
           moe_w2):
    depth = w_in.shape[0]
    assert depth == 2, "layer 0 is dense, layer 1 is MoE and is followed by the final norm"
    bsz, seq, _ = x_prompt.shape
    bd, n_new, _ = x_sample.shape
    n_pool = cache_a_k.shape[1]
    past = page_table.shape[1] * LANES
    n_s = bd * n_new
    assert cache_a_k.shape[2] == LANES and n_new <= SUBLANES

    lw = [_layer_weights(l, w_in, w_out, conv_w, subln, lam_q1, lam_k1, lam_q2, lam_k2) for l in range(depth)]
    ffn = (ffn_w1[0].astype(BF16), ffn_w3[0].astype(BF16), ffn_w2[0].astype(BF16))
    moe = (jnp.pad(moe_router[0], ((0, 0), (0, LANES - N_EXPERTS))),
           jnp.pad(moe_router_b[0], (0, LANES - N_EXPERTS), constant_values=NEG).reshape(1, LANES),
           moe_w1[0].astype(BF16), moe_w3[0].astype(BF16), moe_w2[0].astype(BF16))
    row = lambda v: v.reshape(1, -1)
    tiles_a, tiles_b = _prompt_tiles(rel_bias)
    sbias_a, sbias_b = _sample_bias(rel_bias, past, n_new)
    lp = past + LANES
    sbias_a = sbias_a.reshape(N_HEADS_A * SUBLANES, lp)
    sbias_b = jnp.repeat(sbias_b, 2, axis=0).reshape(2 * N_HEADS_B * SUBLANES, lp)
    feat_major = lambda c: jnp.moveaxis(c, 2, -1).reshape(depth, n_pool, -1, LANES)
    ca_k, ca_v, ca_i, cb_k, cb_v = (feat_major(c) for c in (cache_a_k, cache_a_v, cache_a_kidx, cache_b_k, cache_b_v))

    def ffn_block(l, x):
        if l == 0:
            return _ffn(x, row(norm_ffn[l]), *ffn)
        return _moe_final(x, row(norm_ffn[l]), *moe, row(norm_final))

    def rows_of(pr, b, t):
        def tok_major(a, *s):
            c = a.shape[1]
            return jnp.moveaxis(jnp.moveaxis(a, 1, 0).reshape(c, b, t), 0, -1).reshape(b, t, *s)
        return (tok_major(pr["kat"], N_HEADS_A, HEAD_DIM), tok_major(pr["vat"], N_HEADS_A, HEAD_DIM),
                tok_major(pr["kwt"][:, :IDX_DIM], IDX_DIM), tok_major(pr["kbt"], N_HEADS_B, HEAD_DIM),
                tok_major(pr["vbt"], N_HEADS_B, HEAD_DIM),
                pr["u"].reshape(b, t, D_C)[:, t - (CONV_WIDTH - 1):])

    x = x_prompt.reshape(bsz * seq, D_MODEL)
    rows_p = []
    zeros_state = jnp.zeros((bsz, SUBLANES, D_C), F32)
    for l in range(depth):
        pr = _proj(x, row(norm_mix[l]), lw[l]["w_tok"], lw[l]["w_feat"], bsz, seq)
        rows_p.append(rows_of(pr, bsz, seq))
        oa = _dsa_prompt(pr, tiles_a, bsz, seq)
        ob = _diff_prompt(pr, tiles_b, lw[l]["lam"], lw[l]["g_col"], lw[l]["out_scale"], bsz, seq)
        x = _outproj(x, oa, ob, pr["u"], pr["gb"], lw[l]["cw"], lw[l]["w_out"], pr["u"], zeros_state,
                     seq, False)
        x = ffn_block(l, x)
    y_prompt = x.reshape(bsz, seq, D_MODEL)

    x = x_sample.reshape(n_s, D_MODEL)
    rows_s = []
    for l in range(depth):
        pr = _proj(x, row(norm_mix[l]), lw[l]["w_tok"], lw[l]["w_feat"], 1, n_s)
        rows_s.append(rows_of(pr, bd, n_new))
        r3 = lambda a: a.reshape(bd, n_new, a.shape[-1])
        new_rows = lambda a: _pad_rows(r3(a), LANES)
        new_rows_t = lambda a: _pad_rows(jnp.moveaxis(a[0].reshape(-1, bd, n_new), 0, -1), LANES).astype(BF16)
        qi_s = _pad_rows(r3(pr["qi"]).reshape(bd, n_new, IDX_HEADS, IDX_DIM).transpose(0, 2, 1, 3),
                         SUBLANES).reshape(bd, IDX_HEADS * SUBLANES, IDX_DIM)
        w_tok = jnp.moveaxis(pr["kwt"][0, IDX_DIM:IDX_DIM + IDX_HEADS].reshape(IDX_HEADS, bd, n_new), 0, -1)
        w_s = jnp.pad(_pad_rows(w_tok, SUBLANES) * IDX_HEADS ** -0.5, ((0, 0), (0, 0), (0, LANES - IDX_HEADS)))
        mb = _idx_sample(page_table, qi_s, w_s, new_rows(pr["kib"]), ca_i, l, past, n_new, N_HEADS_A)
        oa = _attn_sample(page_table, _block_diag_queries(r3(pr["qa"]), 1), mb, sbias_a,
                          new_rows(pr["kab"]), new_rows_t(pr["vatb"]), ca_k, ca_v, l, past)
        ob = _attn_sample(page_table, _block_diag_queries(r3(pr["qb"]), 2), None, sbias_b,
                          new_rows(pr["kbb"]), new_rows_t(pr["vbtb"]), cb_k, cb_v, l, past,
                          lam=lw[l]["lam"], g_row=lw[l]["g_row"], diff_scale=lw[l]["out_scale"])
        flat = lambda o: o[:, :n_new].reshape(n_s, o.shape[-1]).astype(BF16)
        st = state_conv[l]
        zero = jnp.zeros((bd, n_new - 1, D_C), F32)
        a = jnp.concatenate([st[:, 1:2], zero], axis=1).reshape(n_s, D_C)
        b = jnp.concatenate([st, zero[:, 1:]], axis=1).reshape(n_s, D_C)
        x = _outproj(x, flat(oa), flat(ob), pr["u"], pr["gb"], lw[l]["cw"], lw[l]["w_out"], a, b,
                     n_new, True)
        x = ffn_block(l, x)
    y_sample = x.reshape(bd, n_new, D_MODEL)

    stack = lambda rows: tuple(jnp.stack([r[i] for r in rows]) for i in range(6))
    return (y_prompt, y_sample) + stack(rows_p) + stack(rows_s)
```

```python
import functools
import math

import jax
import jax.numpy as jnp
from jax import lax
from jax.experimental import pallas as pl
from jax.experimental.pallas import tpu as pltpu

F32 = jnp.float32
BF16 = jnp.bfloat16
I32 = jnp.int32

D_MODEL = 1024
HEAD_DIM = 64
N_HEADS_A = 6
N_HEADS_B = 6
D_A = N_HEADS_A * HEAD_DIM
D_B = N_HEADS_B * HEAD_DIM
D_C = D_MODEL - D_A - D_B
DIFF_DIM = HEAD_DIM // 2
IDX_HEADS = 4
IDX_DIM = 64
TOPK_MAX = 256
CONV_WIDTH = 3
NUM_BUCKETS = 32
MAX_DISTANCE = 128
N_EXPERTS = 8
EPS = 1e-6

LANES = 128
SUBLANES = 8
VMEM_LIMIT = 56 * 1024 * 1024
LOG2E = 1.4426950408889634
NEG = -1e30
INT_MIN = -2**31
BIG_IDX = 2**30
I16 = jnp.int16
MIN16 = -2**15
BIG16 = 2**15 - 1
PACK16 = 16

ATT_TILE = 256
PAGES_PER_STEP = 16
N_PAIRS = D_A // LANES

R_QA, R_QI, R_QB, R_KA, R_KB, R_KI, R_HC, R_GB, R_GC, R_END = (
    0, 384, 640, 1024, 1408, 1792, 1920, 2176, 2432, 2688)
T_KA, T_VA, T_KB, T_VB, T_KW, T_END = 0, 384, 768, 1152, 1536, 1664


def _cparams(sem):
    return pltpu.CompilerParams(dimension_semantics=sem, vmem_limit_bytes=VMEM_LIMIT)


def _dot_t(a, b):
    return lax.dot_general(a, b, (((1,), (1,)), ((), ())), preferred_element_type=F32)


def _dot(a, b):
    return jnp.dot(a, b, preferred_element_type=F32)


def _rms(x, g):
    ms = jnp.mean(x * x, axis=-1, keepdims=True)
    return x * lax.rsqrt(ms + EPS) * g


def _proj_kernel(x_ref, g_ref, w_ref, wt_ref, qa_ref, qi_ref, qb_ref, kab_ref, kbb_ref, kib_ref,
                 u_ref, gb_ref, kat_ref, vat_ref, vatb_ref, kbt_ref, vbt_ref, vbtb_ref, kwt_ref):
    h = _rms(x_ref[...], g_ref[...]).astype(BF16)

    def mm(a, b):
        return _dot(h, w_ref[:, a:b])

    def mt(a, b):
        return _dot_t(wt_ref[a:b, :], h)

    qa_ref[...] = (mm(R_QA, R_QI) * (HEAD_DIM ** -0.5 * LOG2E)).astype(BF16)
    qi_ref[...] = (mm(R_QI, R_QB) * IDX_DIM ** -0.5).astype(BF16)
    qb_ref[...] = (mm(R_QB, R_KA) * (DIFF_DIM ** -0.5 * LOG2E)).astype(BF16)
    kab_ref[...] = mm(R_KA, R_KB).astype(BF16)
    kbb_ref[...] = mm(R_KB, R_KI).astype(BF16)
    kib_ref[...] = mm(R_KI, R_HC)[:, :IDX_DIM].astype(BF16)
    hc = mm(R_HC, R_GB)
    gb_ref[...] = mm(R_GB, R_GC)
    u_ref[...] = mm(R_GC, R_END) * hc
    kat_ref[...] = mt(T_KA, T_VA)
    va = mt(T_VA, T_KB)
    vat_ref[...] = va
    vatb_ref[...] = va.astype(BF16)
    kbt_ref[...] = mt(T_KB, T_VB)
    vb = mt(T_VB, T_KW)
    vbt_ref[...] = vb
    vbtb_ref[...] = vb.astype(BF16)
    kwt_ref[...] = mt(T_KW, T_END)


def _proj(x, g, w, wt, bsz, t):
    tm = min(ATT_TILE, t)
    assert t % tm == 0
    tpb = t // tm
    tok = dict(qa=(D_A, BF16), qi=(IDX_HEADS * IDX_DIM, BF16), qb=(D_B, BF16), kab=(D_A, BF16),
               kbb=(D_B, BF16), kib=(IDX_DIM, BF16), u=(D_C, F32), gb=(D_C, F32))
    feat = dict(kat=(D_A, F32), vat=(D_A, F32), vatb=(D_A, BF16), kbt=(D_B, F32), vbt=(D_B, F32),
                vbtb=(D_B, BF16), kwt=(LANES, F32))
    outs = pl.pallas_call(
        _proj_kernel,
        grid=(bsz * tpb,),
        in_specs=[pl.BlockSpec((tm, D_MODEL), lambda i: (i, 0)),
                  pl.BlockSpec((1, D_MODEL), lambda i: (0, 0)),
                  pl.BlockSpec((D_MODEL, R_END), lambda i: (0, 0)),
                  pl.BlockSpec((T_END, D_MODEL), lambda i: (0, 0))],
        out_specs=[pl.BlockSpec((tm, c), lambda i: (i, 0)) for c, _ in tok.values()]
        + [pl.BlockSpec((None, c, tm), lambda i: (i // tpb, 0, i % tpb)) for c, _ in feat.values()],
        out_shape=[jax.ShapeDtypeStruct((bsz * t, c), d) for c, d in tok.values()]
        + [jax.ShapeDtypeStruct((bsz, c, t), d) for c, d in feat.values()],
        compiler_params=_cparams(("parallel",)),
        name="proj",
    )(x, g, w, wt)
    return dict(zip(list(tok) + list(feat), outs))


def _sortable_key(score):
    score = jnp.where(score == 0.0, 0.0, score)
    bits = pltpu.bitcast(score, I32)
    return bits ^ ((bits >> 31) & 0x7FFFFFFF)


def _topk_search(count, transform, shape, topk, idx_bits):
    kf = float(topk)
    t = jnp.where(count(lambda blk, c: blk >= 0, None) >= kf, 0, INT_MIN).astype(I32)

    def value_bit(i, t):
        cand = t | lax.shift_left(jnp.int32(1), 30 - i)
        return jnp.where(count(lambda blk, c: blk >= c, cand) >= kf, cand, t)

    t = lax.fori_loop(0, 31, value_bit, t)
    transform(jnp.maximum(t, INT_MIN + 1))

    def index_bit(i, m):
        cand = m | lax.shift_left(jnp.int32(1), idx_bits - 1 - i)
        return jnp.where(count(lambda blk, c: blk < c, cand) < kf, cand, m)

    return lax.fori_loop(0, idx_bits, index_bit, jnp.zeros(shape, I32))


def _rewrite_keys(blk, t, idx):
    return jnp.where(blk > t, -1, jnp.where(blk == t, idx, BIG_IDX))


def _tree_sum(parts):
    while len(parts) > 1:
        parts = [a + b for a, b in zip(parts[::2], parts[1::2])] + parts[len(parts) & ~1:]
    return parts[0]


def _topk_search_16(hi_ref, lo_ref, n_ch, tk, topk, idx_bits):
    tq = hi_ref.shape[1]
    one, zero = jnp.int16(1), jnp.int16(0)
    kf = float(topk)
    row = lax.broadcasted_iota(I32, (tk, tq), 0)

    def chunk(kc):
        return pl.ds(pl.multiple_of(kc * tk, tk), tk)

    def count(ref, pred, cand):
        c16 = jnp.broadcast_to(cand, (PACK16, tq)).astype(I16)

        def body(i, acc):
            blks = [ref[chunk(2 * i + s), :] for s in range(2)]
            return acc + _tree_sum([jnp.where(pred(blk[g * PACK16:(g + 1) * PACK16], c16), one, zero)
                                    for blk in blks for g in range(tk // PACK16)])

        acc = lax.fori_loop(0, n_ch // 2, body, jnp.zeros((PACK16, tq), I16))
        return jnp.sum(acc.astype(F32), axis=0, keepdims=True)

    def largest_with(ref, need):
        ge = lambda c: count(ref, lambda b, c16: b >= c16, c)
        t = jnp.where(ge(jnp.zeros((1, tq), I32)) >= need, 0, MIN16).astype(I32)
        return lax.fori_loop(
            0, 15, lambda i, t: jnp.where(ge(t | lax.shift_left(jnp.int32(1), 14 - i)) >= need,
                                          t | lax.shift_left(jnp.int32(1), 14 - i), t), t)

    def full16(v):
        return jnp.broadcast_to(v, (tk, tq)).astype(I16)

    t_hi = jnp.maximum(largest_with(hi_ref, kf), MIN16 + 1)
    need = kf - count(hi_ref, lambda b, c16: b > c16, t_hi)
    thi_f = full16(t_hi)

    def keep_candidates(kc, c):
        lo_ref[chunk(kc), :] = jnp.where(hi_ref[chunk(kc), :] == thi_f, lo_ref[chunk(kc), :], jnp.int16(MIN16))
        return c

    lax.fori_loop(0, n_ch, keep_candidates, 0)
    tlo_f = full16(largest_with(lo_ref, need))
    minus1, big = jnp.int16(-1), jnp.int16(BIG16)

    def rewrite(kc, c):
        hi = hi_ref[chunk(kc), :]
        lo = lo_ref[chunk(kc), :]
        idx = (kc * tk + row).astype(I16)
        inner = jnp.where(lo > tlo_f, minus1, jnp.where(lo == tlo_f, idx, big))
        hi_ref[chunk(kc), :] = jnp.where(hi > thi_f, minus1, jnp.where(hi == thi_f, inner, big))
        return c

    lax.fori_loop(0, n_ch, rewrite, 0)

    def index_bit(i, m):
        cand = m | lax.shift_left(jnp.int32(1), idx_bits - 1 - i)
        return jnp.where(count(hi_ref, lambda b, c16: b < c16, cand) < kf, cand, m)

    return lax.fori_loop(0, idx_bits, index_bit, jnp.zeros((1, tq), I32))


N_TILE_VARIANTS = 4


def _tile_variant(kc, qi):
    return jnp.where(kc == qi, 0, jnp.where(kc == qi - 1, 1, jnp.where(kc > qi, 3, 2)))


def _softmax_step_t(lg_of, vt, m_ref, l_ref, acc_ref):
    alphas, probs = [], []
    for j in range(m_ref.shape[-1] // LANES):
        cols = slice(j * LANES, (j + 1) * LANES)
        lg = lg_of(cols)
        m_old = m_ref[:, cols]
        m_new = jnp.maximum(m_old, jnp.max(lg, axis=0, keepdims=True))
        alpha = jnp.exp2(m_old - m_new)
        p = jnp.exp2(lg - m_new)
        l_ref[:, cols] = alpha * l_ref[:, cols] + jnp.sum(p, axis=0, keepdims=True)
        m_ref[:, cols] = m_new
        alphas.append(alpha)
        probs.append(p.astype(BF16))
    acc_ref[...] = (jnp.concatenate(alphas, axis=1) * acc_ref[...]
                    + _dot(vt, jnp.concatenate(probs, axis=1)))


def _dsa_prompt_kernel(qi_ref, kwt_ref, kib_ref, qa_ref, kab_ref, vat_ref, tiles_ref, oa_ref,
                       hi_sc, lo_sc, mb_sc, lg_sc, qm_sc, m_sc, l_sc, acc_sc, *, topk, idx_bits):
    tq = ATT_TILE
    qi = pl.program_id(1)
    n_ch = qi + 1
    q0 = qi * tq
    row = lax.broadcasted_iota(I32, (tq, tq), 0)
    col = lax.broadcasted_iota(I32, (tq, tq), 1)

    w4 = kwt_ref[IDX_DIM:IDX_DIM + SUBLANES, :] * IDX_HEADS ** -0.5

    n_trips = (n_ch + 1) // 2

    def score_chunks(i, c):
        for s in range(2):
            kc = 2 * i + s
            k0 = pl.multiple_of(kc * tq, tq)
            kch = kib_ref[pl.ds(pl.multiple_of(jnp.minimum(kc, pl.num_programs(1) - 1) * tq, tq), tq), :]
            sc = jnp.zeros((tq, tq), F32)
            for h in range(IDX_HEADS):
                sc = sc + w4[h:h + 1, :] * jnp.maximum(_dot_t(kch, qi_ref[:, h * IDX_DIM:(h + 1) * IDX_DIM]), 0.0)
            valid = (k0 + row) <= (q0 + col)
            key = jnp.where(valid, _sortable_key(sc), INT_MIN)
            hi_sc[pl.ds(k0, tq), :] = (key >> 16).astype(I16)
            lo_sc[pl.ds(k0, tq), :] = ((key & 0xFFFF) + MIN16).astype(I16)
        return c

    lax.fori_loop(0, n_trips, score_chunks, 0)

    thr = _topk_search_16(hi_sc, lo_sc, 2 * n_trips, tq, topk, idx_bits)

    lane = lax.broadcasted_iota(I32, (tq, LANES), 1)
    for p in range(N_PAIRS):
        qp = qa_ref[:, p * LANES:(p + 1) * LANES]
        qm_sc[p, 0:tq, :] = jnp.where(lane < HEAD_DIM, qp, jnp.zeros_like(qp))
        qm_sc[p, tq:2 * tq, :] = jnp.where(lane >= HEAD_DIM, qp, jnp.zeros_like(qp))
    m_sc[...] = jnp.full(m_sc.shape, NEG, F32)
    l_sc[...] = jnp.zeros(l_sc.shape, F32)
    acc_sc[...] = jnp.zeros(acc_sc.shape, F32)

    def logits_into(slot, kc):
        k0 = pl.multiple_of(kc * tq, tq)
        for p in range(N_PAIRS):
            kp = kab_ref[pl.ds(k0, tq), p * LANES:(p + 1) * LANES]
            lg_sc[slot, :, 2 * p * tq:2 * (p + 1) * tq] = _dot_t(kp, qm_sc[p])

    last = pl.num_programs(1) - 1
    logits_into(0, 0)
    logits_into(1, jnp.minimum(1, last))

    def attend(i, c, near):
        for s in range(2):
            kc = 2 * i + s
            k0 = pl.multiple_of(jnp.minimum(kc, last) * tq, tq)
            var = _tile_variant(kc, qi)
            mb_sc[s] = jnp.where(hi_sc[pl.ds(k0, tq), :].astype(I32) <= thr, 0.0, NEG)
            for h in range(N_HEADS_A):
                p = h // 2
                vt = vat_ref[p * LANES:(p + 1) * LANES, pl.ds(k0, tq)]

                def lg_of(cols, h=h):
                    shifted = slice(h * tq + cols.start, h * tq + cols.stop)
                    lg = lg_sc[s, :, shifted] + mb_sc[s, :, cols]
                    return lg + tiles_ref[var, h, :, cols] if near else lg

                _softmax_step_t(lg_of, vt, m_sc.at[h], l_sc.at[h], acc_sc.at[h])
            logits_into(s, jnp.minimum(kc + 2, last))
        return c

    n_far = jnp.maximum(qi - 1, 0) // 2
    lax.fori_loop(0, n_far, functools.partial(attend, near=False), 0)
    lax.fori_loop(n_far, n_trips, functools.partial(attend, near=True), 0)

    srow = lax.broadcasted_iota(I32, (LANES, tq), 0)
    for p in range(N_PAIRS):
        o_lo = acc_sc[2 * p] / l_sc[2 * p]
        o_hi = acc_sc[2 * p + 1] / l_sc[2 * p + 1]
        o_t = jnp.where(srow < HEAD_DIM, o_lo, o_hi)
        oa_ref[:, p * LANES:(p + 1) * LANES] = o_t.T.astype(BF16)


def _dsa_prompt(pr, tiles, bsz, t):
    tq = ATT_TILE
    assert t % tq == 0
    topk = min(TOPK_MAX, t // 4)
    idx_bits = max(1, (t - 1).bit_length())
    r3 = lambda a: a.reshape(bsz, t, a.shape[-1])
    qtile = lambda c: pl.BlockSpec((None, tq, c), lambda b, i: (b, i, 0))
    full = lambda c: pl.BlockSpec((None, t, c), lambda b, i: (b, 0, 0))
    out = pl.pallas_call(
        functools.partial(_dsa_prompt_kernel, topk=topk, idx_bits=idx_bits),
        grid=(bsz, t // tq),
        in_specs=[qtile(IDX_HEADS * IDX_DIM),
                  pl.BlockSpec((None, LANES, tq), lambda b, i: (b, 0, i)),
                  full(IDX_DIM), qtile(D_A), full(D_A),
                  pl.BlockSpec((None, D_A, t), lambda b, i: (b, 0, 0)),
                  pl.BlockSpec((N_TILE_VARIANTS, N_HEADS_A, tq, tq), lambda b, i: (0, 0, 0, 0))],
        out_specs=qtile(D_A),
        out_shape=jax.ShapeDtypeStruct((bsz, t, D_A), BF16),
        scratch_shapes=[pltpu.VMEM((t + tq, tq), I16), pltpu.VMEM((t + tq, tq), I16),
                        pltpu.VMEM((2, tq, tq), F32),
                        pltpu.VMEM((2, tq, N_HEADS_A * tq), F32),
                        pltpu.VMEM((N_PAIRS, 2 * tq, LANES), BF16),
                        pltpu.VMEM((N_HEADS_A, 1, tq), F32), pltpu.VMEM((N_HEADS_A, 1, tq), F32),
                        pltpu.VMEM((N_HEADS_A, LANES, tq), F32)],
        compiler_params=_cparams(("parallel", "arbitrary")),
        name="dsa_prompt",
    )(r3(pr["qi"]), pr["kwt"], r3(pr["kib"]), r3(pr["qa"]), r3(pr["kab"]), pr["vatb"], tiles)
    return out.reshape(bsz * t, D_A)


def _diff_prompt_kernel(lam_ref, qb_ref, kbb_ref, vbt_ref, tiles_ref, g_ref, ob_ref,
                        lg_sc, qm_sc, m_sc, l_sc, acc_sc, *, out_scale):
    tq = ATT_TILE
    qi = pl.program_id(1)
    lane = lax.broadcasted_iota(I32, (tq, LANES), 1)
    for p in range(N_PAIRS):
        qp = qb_ref[:, p * LANES:(p + 1) * LANES]
        for v in range(4):
            inside = (lane >= v * DIFF_DIM) & (lane < (v + 1) * DIFF_DIM)
            qm_sc[p, v * tq:(v + 1) * tq, :] = jnp.where(inside, qp, jnp.zeros_like(qp))
    m_sc[...] = jnp.full(m_sc.shape, NEG, F32)
    l_sc[...] = jnp.zeros(l_sc.shape, F32)
    acc_sc[...] = jnp.zeros(acc_sc.shape, F32)

    n_ch = qi + 1

    def logits_into(slot, kc):
        k0 = pl.multiple_of(kc * tq, tq)
        for p in range(N_PAIRS):
            kp = kbb_ref[pl.ds(k0, tq), p * LANES:(p + 1) * LANES]
            for hh in range(2):
                u0 = (4 * p + 2 * hh) * tq
                lg_sc[slot, :, u0:u0 + 2 * tq] = _dot_t(kp, qm_sc[p, 2 * hh * tq:2 * (hh + 1) * tq, :])

    last = pl.num_programs(1) - 1
    logits_into(0, 0)
    logits_into(1, jnp.minimum(1, last))

    def attend(i, c, near):
        for s in range(2):
            kc = 2 * i + s
            k0 = pl.multiple_of(jnp.minimum(kc, last) * tq, tq)
            var = _tile_variant(kc, qi)
            for p in range(N_PAIRS):
                vt = vbt_ref[p * LANES:(p + 1) * LANES, pl.ds(k0, tq)]
                for v in range(4):
                    def lg_of(cols, u=4 * p + v, h=2 * p + v // 2):
                        shifted = slice(u * tq + cols.start, u * tq + cols.stop)
                        lg = lg_sc[s, :, shifted]
                        return lg + tiles_ref[var, h, :, cols] if near else lg

                    _softmax_step_t(lg_of, vt, m_sc.at[p, v], l_sc.at[p, v], acc_sc.at[p, v])
            logits_into(s, jnp.minimum(kc + 2, last))
        return c

    n_far = jnp.maximum(qi - 1, 0) // 2
    lax.fori_loop(0, n_far, functools.partial(attend, near=False), 0)
    lax.fori_loop(n_far, (n_ch + 1) // 2, functools.partial(attend, near=True), 0)

    lam = lam_ref[0]
    lo = lax.broadcasted_iota(I32, (LANES, tq), 0) < HEAD_DIM
    for p in range(N_PAIRS):
        def head(v1, v2):
            return acc_sc[p, v1] / l_sc[p, v1] - lam * (acc_sc[p, v2] / l_sc[p, v2])

        o = jnp.where(lo, head(0, 1), head(2, 3))
        sq = o * o
        ms_lo = jnp.sum(jnp.where(lo, sq, 0.0), axis=0, keepdims=True) * (1.0 / HEAD_DIM)
        ms_hi = jnp.sum(jnp.where(lo, 0.0, sq), axis=0, keepdims=True) * (1.0 / HEAD_DIM)
        r = jnp.where(lo, lax.rsqrt(ms_lo + EPS), lax.rsqrt(ms_hi + EPS))
        y = o * r * g_ref[...] * out_scale
        ob_ref[:, p * LANES:(p + 1) * LANES] = y.T.astype(BF16)


def _diff_prompt(pr, tiles, lam, g_col, out_scale, bsz, t):
    tq = ATT_TILE
    r3 = lambda a: a.reshape(bsz, t, a.shape[-1])
    qtile = lambda c: pl.BlockSpec((None, tq, c), lambda b, i: (b, i, 0))
    full = lambda c: pl.BlockSpec((None, t, c), lambda b, i: (b, 0, 0))
    out = pl.pallas_call(
        functools.partial(_diff_prompt_kernel, out_scale=out_scale),
        grid=(bsz, t // tq),
        in_specs=[pl.BlockSpec(memory_space=pltpu.SMEM), qtile(D_B), full(D_B),
                  pl.BlockSpec((None, D_B, t), lambda b, i: (b, 0, 0)),
                  pl.BlockSpec((N_TILE_VARIANTS, N_HEADS_B, tq, tq), lambda b, i: (0, 0, 0, 0)),
                  pl.BlockSpec((LANES, 1), lambda b, i: (0, 0))],
        out_specs=qtile(D_B),
        out_shape=jax.ShapeDtypeStruct((bsz, t, D_B), BF16),
        scratch_shapes=[pltpu.VMEM((2, tq, 2 * N_HEADS_B * tq), F32),
                        pltpu.VMEM((N_PAIRS, 4 * tq, LANES), BF16),
                        pltpu.VMEM((N_PAIRS, 4, 1, tq), F32), pltpu.VMEM((N_PAIRS, 4, 1, tq), F32),
                        pltpu.VMEM((N_PAIRS, 4, LANES, tq), F32)],
        compiler_params=_cparams(("parallel", "arbitrary")),
        name="diff_prompt",
    )(lam, r3(pr["qb"]), r3(pr["kbb"]), pr["vbtb"], tiles, g_col)
    return out.reshape(bsz * t, D_B)


def _gather_chunk(page_refs):
    return jnp.concatenate([r[...] for r in page_refs], axis=1).astype(BF16)


def _idx_sample_kernel(pt_ref, qi_ref, w_ref, kin_ref, *rest, past, topk, idx_bits, n_rep):
    del pt_ref
    pages = rest[:PAGES_PER_STEP]
    mb_ref, s_sc = rest[PAGES_PER_STEP:]
    g = pl.program_id(1)
    ck = PAGES_PER_STEP * LANES
    w4 = w_ref[...]

    def scores(s_all):
        sc = jnp.zeros((SUBLANES, s_all.shape[1]), F32)
        for h in range(IDX_HEADS):
            sc = sc + w4[:, h:h + 1] * jnp.maximum(s_all[h * SUBLANES:(h + 1) * SUBLANES], 0.0)
        return _sortable_key(sc)

    s_sc[:, pl.ds(pl.multiple_of(g * ck, ck), ck)] = scores(_dot(qi_ref[...], _gather_chunk(pages)))

    @pl.when(g == pl.num_programs(1) - 1)
    def _():
        row = lax.broadcasted_iota(I32, (SUBLANES, LANES), 0)
        col = lax.broadcasted_iota(I32, (SUBLANES, LANES), 1)
        new = scores(_dot_t(qi_ref[...], kin_ref[...]))
        s_sc[:, past:past + LANES] = jnp.where(col <= row, new, INT_MIN)
        n_groups = past // LANES + 1

        def group(i):
            return slice(i * LANES, (i + 1) * LANES)

        def count(pred, cand):
            cb = None if cand is None else jnp.broadcast_to(cand, (SUBLANES, LANES))
            parts = [jnp.where(pred(s_sc[:, group(i)], cb), 1, 0) for i in range(n_groups)]
            while len(parts) > 1:
                parts = [a + b for a, b in zip(parts[::2], parts[1::2])] + parts[len(parts) & ~1:]
            return jnp.sum(parts[0].astype(F32), axis=1, keepdims=True)

        def transform(t):
            for i in range(n_groups):
                s_sc[:, group(i)] = _rewrite_keys(s_sc[:, group(i)], t, i * LANES + col)

        m = _topk_search(count, transform, (SUBLANES, 1), topk, idx_bits)
        mb = jnp.where(s_sc[...] <= m, 0.0, NEG)
        mb_ref[...] = jnp.concatenate([mb] * n_rep, axis=0)


def _page_specs(width, layer, n):
    return [pl.BlockSpec((None, None, width, LANES),
                         functools.partial(lambda b, g, pt, j: (layer, pt[b, g * PAGES_PER_STEP + j], 0, 0), j=j))
            for j in range(n)]


def _idx_sample(page_table, qi_s, w_s, kin_s, cache_kidx_t, layer, past, n_new, n_rep):
    bd = qi_s.shape[0]
    lp = past + LANES
    n_steps = past // (PAGES_PER_STEP * LANES)
    topk = min(TOPK_MAX, (past + n_new) // 4)
    idx_bits = (past + n_new - 1).bit_length()
    per_b = lambda r, c: pl.BlockSpec((None, r, c), lambda b, g, pt: (b, 0, 0))
    return pl.pallas_call(
        functools.partial(_idx_sample_kernel, past=past, topk=topk, idx_bits=idx_bits, n_rep=n_rep),
        grid_spec=pltpu.PrefetchScalarGridSpec(
            num_scalar_prefetch=1, grid=(bd, n_steps),
            in_specs=[per_b(IDX_HEADS * SUBLANES, IDX_DIM), per_b(SUBLANES, LANES), per_b(LANES, IDX_DIM)]
            + _page_specs(IDX_DIM, layer, PAGES_PER_STEP),
            out_specs=per_b(n_rep * SUBLANES, lp),
            scratch_shapes=[pltpu.VMEM((SUBLANES, lp), I32)]),
        out_shape=jax.ShapeDtypeStruct((bd, n_rep * SUBLANES, lp), F32),
        compiler_params=_cparams(("parallel", "arbitrary")),
        name="idx_sample",
    )(page_table, qi_s, w_s, kin_s, *([cache_kidx_t] * PAGES_PER_STEP))


def _attn_sample_kernel(pt_ref, *refs, past, use_mask, diff_scale):
    del pt_ref
    refs = list(refs)
    lam_ref = refs.pop(0) if diff_scale is not None else None
    q_ref = refs.pop(0)
    mb_ref = refs.pop(0) if use_mask else None
    bias_ref, kn_ref, vn_ref = refs[:3]
    refs = refs[3:]
    g_ref = refs.pop(0) if diff_scale is not None else None
    kpages = refs[:PAGES_PER_STEP]
    vpages = refs[PAGES_PER_STEP:2 * PAGES_PER_STEP]
    o_ref, m_sc, l_sc, acc_sc = refs[2 * PAGES_PER_STEP:]
    g = pl.program_id(1)
    ck = PAGES_PER_STEP * LANES

    @pl.when(g == 0)
    def _():
        m_sc[...] = jnp.full(m_sc.shape, NEG, F32)
        l_sc[...] = jnp.zeros(l_sc.shape, F32)
        acc_sc[...] = jnp.zeros(acc_sc.shape, F32)

    def attend(lg, c0, width, pv):
        lg = lg + bias_ref[:, pl.ds(c0, width)]
        if use_mask:
            lg = lg + mb_ref[:, pl.ds(c0, width)]
        m_old = m_sc[...]
        m_new = jnp.maximum(m_old, jnp.max(lg, axis=1, keepdims=True))
        alpha = jnp.exp2(m_old - m_new)
        p = jnp.exp2(lg - m_new)
        l_sc[...] = alpha * l_sc[...] + jnp.sum(p, axis=1, keepdims=True)
        acc_sc[...] = alpha * acc_sc[...] + pv(p.astype(BF16))
        m_sc[...] = m_new

    q = q_ref[...]
    vt = _gather_chunk(vpages)
    attend(_dot(q, _gather_chunk(kpages)), pl.multiple_of(g * ck, ck), ck, lambda p: _dot_t(p, vt))

    @pl.when(g == pl.num_programs(1) - 1)
    def _():
        attend(_dot_t(q, kn_ref[...]), past, LANES, lambda p: _dot(p, vn_ref[...]))
        head_of_col = lax.broadcasted_iota(I32, (SUBLANES, D_A), 1) // HEAD_DIM
        n_comp = q.shape[0] // (N_HEADS_A * SUBLANES)

        def gather_heads(comp):
            out = jnp.zeros((SUBLANES, D_A), F32)
            for h in range(N_HEADS_A):
                r0 = (h * n_comp + comp) * SUBLANES
                o_h = acc_sc[r0:r0 + SUBLANES, :] / l_sc[r0:r0 + SUBLANES, :]
                out = jnp.where(head_of_col == h, o_h, out)
            return out

        if diff_scale is None:
            o_ref[...] = gather_heads(0)
        else:
            o = gather_heads(0) - lam_ref[0] * gather_heads(1)
            sq = o * o
            r = jnp.zeros((SUBLANES, D_A), F32)
            for h in range(N_HEADS_B):
                ms = jnp.sum(jnp.where(head_of_col == h, sq, 0.0), axis=1, keepdims=True) * (1.0 / HEAD_DIM)
                r = jnp.where(head_of_col == h, lax.rsqrt(ms + EPS), r)
            o_ref[...] = o * r * g_ref[...] * diff_scale


def _attn_sample(page_table, q_bd, mb, bias_s, kn_s, vn_s, cache_kt, cache_vt, layer, past,
                 lam=None, g_row=None, diff_scale=None):
    bd, rows, _ = q_bd.shape
    lp = past + LANES
    n_steps = past // (PAGES_PER_STEP * LANES)
    use_mask = mb is not None
    per_b = lambda *s: pl.BlockSpec((None,) + s, lambda b, g, pt: (b,) + (0,) * len(s))
    const = lambda *s: pl.BlockSpec(s, lambda b, g, pt: (0,) * len(s))
    args, specs = [], []
    if diff_scale is not None:
        args.append(lam)
        specs.append(pl.BlockSpec(memory_space=pltpu.SMEM))
    args.append(q_bd)
    specs.append(per_b(rows, D_A))
    if use_mask:
        args.append(mb)
        specs.append(per_b(rows, lp))
    args += [bias_s, kn_s, vn_s]
    specs += [const(rows, lp), per_b(LANES, D_A), per_b(LANES, D_A)]
    if diff_scale is not None:
        args.append(g_row)
        specs.append(const(1, D_A))
    args += [cache_kt] * PAGES_PER_STEP + [cache_vt] * PAGES_PER_STEP
    specs += _page_specs(D_A, layer, PAGES_PER_STEP) + _page_specs(D_A, layer, PAGES_PER_STEP)
    return pl.pallas_call(
        functools.partial(_attn_sample_kernel, past=past, use_mask=use_mask, diff_scale=diff_scale),
        grid_spec=pltpu.PrefetchScalarGridSpec(
            num_scalar_prefetch=1, grid=(bd, n_steps), in_specs=specs,
            out_specs=per_b(SUBLANES, D_A),
            scratch_shapes=[pltpu.VMEM((rows, 1), F32), pltpu.VMEM((rows, 1), F32),
                            pltpu.VMEM((rows, D_A), F32)]),
        out_shape=jax.ShapeDtypeStruct((bd, SUBLANES, D_A), F32),
        compiler_params=_cparams(("parallel", "arbitrary")),
        name="attn_sample_diff" if diff_scale is not None else "attn_sample_dsa",
    )(page_table, *args)


def _outproj_kernel(x_ref, oa_ref, ob_ref, u_ref, a_ref, b_ref, gb_ref, cw_ref, wo_ref, o_ref,
                    *, seq, per_row_state):
    tm = x_ref.shape[0]
    u = u_ref[...]
    row = lax.broadcasted_iota(I32, (tm, D_C), 0)
    um1 = pltpu.roll(u, 1, axis=0)
    um2 = pltpu.roll(u, 2, axis=0)
    if per_row_state:
        t = row % seq
        um1 = jnp.where(t == 0, a_ref[...], um1)
        um2 = jnp.where(t < 2, b_ref[...], um2)
    else:
        first = (pl.program_id(0) % (seq // tm)) == 0
        prev = jnp.where(first, b_ref[...], a_ref[...])
        p6 = prev[SUBLANES - 2:SUBLANES - 1, :]
        p7 = prev[SUBLANES - 1:SUBLANES, :]
        um1 = jnp.where(row == 0, p7, um1)
        um2 = jnp.where(row == 0, p6, jnp.where(row == 1, p7, um2))
    y = cw_ref[0:1, :] * um2 + cw_ref[1:2, :] * um1 + cw_ref[2:3, :] * u
    oc = (gb_ref[...] * y).astype(BF16)
    mixed = (_dot(oa_ref[...], wo_ref[0:D_A, :]) + _dot(ob_ref[...], wo_ref[D_A:D_A + D_B, :])
             + _dot(oc, wo_ref[D_A + D_B:D_MODEL, :]))
    o_ref[...] = x_ref[...] + mixed


def _outproj(x, oa, ob, u, gb, cw, wo, a, b, seq, per_row_state):
    n = x.shape[0]
    tm = min(512, n)
    assert n % tm == 0
    rowt = lambda c: pl.BlockSpec((tm, c), lambda i: (i, 0))
    if per_row_state:
        a_spec, b_spec = rowt(D_C), rowt(D_C)
    else:
        assert seq % tm == 0
        a_spec = pl.BlockSpec((SUBLANES, D_C), lambda i: (jnp.maximum(i * (tm // SUBLANES) - 1, 0), 0))
        b_spec = pl.BlockSpec((None, SUBLANES, D_C), lambda i: (i // (seq // tm), 0, 0))
    return pl.pallas_call(
        functools.partial(_outproj_kernel, seq=seq, per_row_state=per_row_state),
        grid=(n // tm,),
        in_specs=[rowt(D_MODEL), rowt(D_A), rowt(D_B), rowt(D_C), a_spec, b_spec, rowt(D_C),
                  pl.BlockSpec((SUBLANES, D_C), lambda i: (0, 0)),
                  pl.BlockSpec((D_MODEL, D_MODEL), lambda i: (0, 0))],
        out_specs=rowt(D_MODEL),
        out_shape=jax.ShapeDtypeStruct((n, D_MODEL), F32),
        compiler_params=_cparams(("parallel",)),
        name="outproj",
    )(x, oa, ob, u, a, b, gb, cw, wo)


def _silu(a):
    return a / (1.0 + jnp.exp(-a))


def _ffn_kernel(x_ref, g_ref, w1_ref, w3_ref, w2_ref, o_ref, *, n_chunks):
    x = x_ref[...]
    h = _rms(x, g_ref[...]).astype(BF16)
    tf = w1_ref.shape[1] // n_chunks
    out = x
    for c in range(n_chunks):
        a1 = _dot(h, w1_ref[:, c * tf:(c + 1) * tf])
        a3 = _dot(h, w3_ref[:, c * tf:(c + 1) * tf])
        out = out + _dot((_silu(a1) * a3).astype(BF16), w2_ref[c * tf:(c + 1) * tf, :])
    o_ref[...] = out


def _ffn(x, g, w1, w3, w2):
    n = x.shape[0]
    f = w1.shape[1]
    tm = min(256, n)
    n_chunks = 2 if f % (2 * LANES) == 0 else 1
    const = lambda s: pl.BlockSpec(s, lambda i: (0, 0), pipeline_mode=pl.Buffered(1))
    return pl.pallas_call(
        functools.partial(_ffn_kernel, n_chunks=n_chunks),
        grid=(n // tm,),
        in_specs=[pl.BlockSpec((tm, D_MODEL), lambda i: (i, 0)), pl.BlockSpec((1, D_MODEL), lambda i: (0, 0)),
                  const((D_MODEL, f)), const((D_MODEL, f)), const((f, D_MODEL))],
        out_specs=pl.BlockSpec((tm, D_MODEL), lambda i: (i, 0)),
        out_shape=jax.ShapeDtypeStruct((n, D_MODEL), F32),
        compiler_params=_cparams(("parallel",)),
        name="ffn",
    )(x, g, w1, w3, w2)


def _moe_kernel(x_ref, g_ref, wr_ref, br_ref, w1_ref, w3_ref, w2_ref, gf_ref, o_ref,
                h_sc, gate_sc, rank_sc, rank_t_sc, xg_sc, og_sc, acc_sc, *, cs):
    e = pl.program_id(1)
    f = pl.program_id(2)
    tm = x_ref.shape[0]
    lane = lax.broadcasted_iota(I32, (tm, LANES), 1)

    @pl.when((e == 0) & (f == 0))
    def _():
        h = _rms(x_ref[...], g_ref[...])
        h_sc[...] = h.astype(BF16)
        logits = jnp.dot(h, wr_ref[...], preferred_element_type=F32,
                         precision=lax.Precision.HIGHEST) + br_ref[...]
        lane_f = lane.astype(F32)
        top1 = jnp.max(logits, axis=1, keepdims=True)
        i1 = jnp.min(jnp.where(logits == top1, lane_f, float(LANES)), axis=1, keepdims=True)
        rest = jnp.where(lane_f == i1, NEG, logits)
        top2 = jnp.max(rest, axis=1, keepdims=True)
        i2 = jnp.min(jnp.where(rest == top2, lane_f, float(LANES)), axis=1, keepdims=True)
        e2 = jnp.exp(top2 - top1)
        den = 1.0 + e2
        gate_sc[...] = jnp.where(lane_f == i1, 1.0 / den, 0.0) + jnp.where(lane_f == i2, e2 / den, 0.0)
        sel = (lane_f == i1) | (lane_f == i2)
        sel_b = jnp.where(sel, 1.0, 0.0).astype(BF16)
        rb = min(ATT_TILE, tm)
        for r0 in range(0, tm, rb):
            earlier = (lax.broadcasted_iota(I32, (rb, tm), 1)
                       < lax.broadcasted_iota(I32, (rb, tm), 0) + r0)
            rank_sc[r0:r0 + rb, :] = jnp.where(sel[r0:r0 + rb], _dot(jnp.where(earlier, 1.0, 0.0).astype(BF16), sel_b), -1.0)
        rank_t_sc[...] = rank_sc[...].T[0:SUBLANES, :]
        acc_sc[...] = jnp.zeros(acc_sc.shape, F32)

    rank = rank_sc[...]
    n_e = jnp.sum(jnp.where((lane == e) & (rank >= 0.0), 1.0, 0.0)).astype(I32)
    n_c = (n_e + (cs - 1)) // cs

    def slots(c):
        return pl.ds(pl.multiple_of(c * cs, cs), cs)

    @pl.when(f == 0)
    def _():
        rank_row = rank_t_sc[pl.ds(e, 1), :]
        slot = lax.broadcasted_iota(I32, (cs, tm), 0).astype(F32)

        def gather(c, carry):
            pick = jnp.where(rank_row == slot + (c * cs).astype(F32), 1.0, 0.0).astype(BF16)
            xg_sc[slots(c), :] = _dot(pick, h_sc[...]).astype(BF16)
            og_sc[slots(c), :] = jnp.zeros((cs, D_MODEL), F32)
            return carry

        lax.fori_loop(0, n_c, gather, 0)

    def expert_chunk(c, carry):
        xg = xg_sc[slots(c), :]
        act = (_silu(_dot(xg, w1_ref[...])) * _dot(xg, w3_ref[...])).astype(BF16)
        og_sc[slots(c), :] += _dot(act, w2_ref[...])
        return carry

    lax.fori_loop(0, n_c, expert_chunk, 0)

    @pl.when(f == pl.num_programs(2) - 1)
    def _():
        rank_col = jnp.sum(jnp.where(lane == e, rank, 0.0), axis=1, keepdims=True)
        gate_col = jnp.sum(jnp.where(lane == e, gate_sc[...], 0.0), axis=1, keepdims=True)
        slot = lax.broadcasted_iota(I32, (tm, cs), 1).astype(F32)

        def scatter(c, carry):
            place = jnp.where(rank_col == slot + (c * cs).astype(F32), 1.0, 0.0).astype(BF16)
            acc_sc[...] += gate_col * _dot(place, og_sc[slots(c), :].astype(BF16))
            return carry

        lax.fori_loop(0, n_c, scatter, 0)

    @pl.when((e == pl.num_programs(1) - 1) & (f == pl.num_programs(2) - 1))
    def _():
        o_ref[...] = _rms(x_ref[...] + acc_sc[...], gf_ref[...])


def _moe_final(x, g, wr, br, w1, w3, w2, g_final):
    n = x.shape[0]
    n_exp, _, f = w1.shape
    tm = min(1024, n)
    cs = min(3 * LANES, tm)
    cap = pl.cdiv(tm, cs) * cs
    n_chunks = 2 if f % (2 * LANES) == 0 else 1
    tf = f // n_chunks
    tok = lambda c: pl.BlockSpec((tm, c), lambda i, e, j: (i, 0))
    const = lambda r, c: pl.BlockSpec((r, c), lambda i, e, j: (0, 0))
    return pl.pallas_call(
        functools.partial(_moe_kernel, cs=cs),
        grid=(n // tm, n_exp, n_chunks),
        in_specs=[tok(D_MODEL), const(1, D_MODEL), const(D_MODEL, LANES), const(1, LANES),
                  pl.BlockSpec((None, D_MODEL, tf), lambda i, e, j: (e, 0, j)),
                  pl.BlockSpec((None, D_MODEL, tf), lambda i, e, j: (e, 0, j)),
                  pl.BlockSpec((None, tf, D_MODEL), lambda i, e, j: (e, j, 0)),
                  const(1, D_MODEL)],
        out_specs=tok(D_MODEL),
        out_shape=jax.ShapeDtypeStruct((n, D_MODEL), F32),
        scratch_shapes=[pltpu.VMEM((tm, D_MODEL), BF16), pltpu.VMEM((tm, LANES), F32),
                        pltpu.VMEM((tm, LANES), F32), pltpu.VMEM((SUBLANES, tm), F32),
                        pltpu.VMEM((cap, D_MODEL), BF16), pltpu.VMEM((cap, D_MODEL), F32),
                        pltpu.VMEM((tm, D_MODEL), F32)],
        compiler_params=_cparams(("parallel", "arbitrary", "arbitrary")),
        name="moe_final",
    )(x, g, wr, br, w1, w3, w2, g_final)


def _t5_bucket(n):
    max_exact = NUM_BUCKETS // 2
    nf = jnp.maximum(n, max_exact).astype(F32)
    large = max_exact + (jnp.log(nf / max_exact) / math.log(MAX_DISTANCE / max_exact)
                         * (NUM_BUCKETS - max_exact)).astype(I32)
    return jnp.where(n < max_exact, n, jnp.minimum(large, NUM_BUCKETS - 1))


def _bias_of_distance(rel_bias, d):
    buckets = _t5_bucket(jnp.arange(MAX_DISTANCE, dtype=I32))
    tab = (rel_bias * LOG2E).T.reshape((rel_bias.shape[1],) + (1,) * d.ndim + (NUM_BUCKETS,))
    out = jnp.broadcast_to(tab[..., 0], (rel_bias.shape[1],) + d.shape)
    for b in range(1, NUM_BUCKETS):
        out = jnp.where(d >= jnp.sum(buckets < b), tab[..., b], out)
    return jnp.where(d < 0, NEG, out)


def _prompt_tiles(rel_bias):
    i = jnp.arange(ATT_TILE, dtype=I32)
    d = i[None, :] - i[:, None]
    far = _bias_of_distance(rel_bias, jnp.full((1, 1), MAX_DISTANCE, I32))
    tiles = jnp.stack([jnp.where(dv < 0, NEG, _bias_of_distance(rel_bias, dv) - far)
                       for dv in (d, d + ATT_TILE, d + 2 * ATT_TILE, d - ATT_TILE)])
    return tiles[:, :N_HEADS_A], tiles[:, N_HEADS_A:]


def _sample_bias(rel_bias, past, n_new):
    i = jnp.arange(SUBLANES, dtype=I32)[:, None]
    k = jnp.arange(past + LANES, dtype=I32)[None, :]
    b = _bias_of_distance(rel_bias, past + i - k)
    b = jnp.where(k < past + n_new, b, NEG)
    return b[:N_HEADS_A], b[N_HEADS_A:]


def _pad_rows(a, rows):
    pad = [(0, 0)] * a.ndim
    pad[-2] = (0, rows - a.shape[-2])
    return jnp.pad(a, pad)


def _block_diag_queries(q, n_comp):
    bd = q.shape[0]
    qp = _pad_rows(q, SUBLANES)[:, None]
    width = HEAD_DIM // n_comp
    owner = jnp.arange(D_A) // width
    blocks = jnp.arange(N_HEADS_A * n_comp)
    keep = (owner[None, :] == blocks[:, None])[None, :, None, :]
    return jnp.where(keep, qp, jnp.zeros_like(qp)).reshape(bd, N_HEADS_A * n_comp * SUBLANES, D_A)


def _layer_weights(l, w_in, w_out, conv_w, subln, lam_q1, lam_k1, lam_q2, lam_k2):
    w = w_in[l]
    qa, ka, va, qi, kw, qb, kb, vb, hc, gb, gc = (
        w[:, a:b] for a, b in ((0, 384), (384, 768), (768, 1152), (1152, 1408), (1408, 1476),
                               (1476, 1860), (1860, 2244), (2244, 2628), (2628, 2884), (2884, 3140),
                               (3140, 3396)))
    kw = jnp.pad(kw, ((0, 0), (0, LANES - kw.shape[1])))
    w_tok = jnp.concatenate([qa, qi, qb, ka, kb, kw, hc, gb, gc], axis=1).astype(BF16)
    w_feat = jnp.concatenate([ka, va, kb, vb, kw], axis=1).T.astype(BF16)
    lam_init = 0.8 - 0.6 * math.exp(-0.3 * l)
    lam = (jnp.exp(jnp.sum(lam_q1[l] * lam_k1[l])) - jnp.exp(jnp.sum(lam_q2[l] * lam_k2[l])) + lam_init)
    return dict(w_tok=w_tok, w_feat=w_feat, w_out=w_out[l].astype(BF16), cw=_pad_rows(conv_w[l], SUBLANES),
                g_col=jnp.tile(subln[l], 2).reshape(LANES, 1), g_row=jnp.tile(subln[l], N_HEADS_B).reshape(1, D_B),
                lam=lam.reshape(1).astype(F32), out_scale=1.0 - lam_init)


def kernel(x_prompt, x_sample, cache_a_k, cache_a_v, cache_a_kidx, cache_b_k, cache_b_v, state_conv,
           page_table, w_in, w_out, norm_mix, norm_ffn, norm_final, rel_bias, lam_q1, lam_k1, lam_q2,
           lam_k2, subln, conv_w, ffn_w1, ffn_w3, ffn_w2, moe_router, moe_router_b, moe_w1, moe_w3,
           moe_w2):
    depth = w_in.shape[0]
    assert depth == 2, "layer 0 is dense, layer 1 is MoE and is followed by the final norm"
    bsz, seq, _ = x_prompt.shape
    bd, n_new, _ = x_sample.shape
    n_pool = cache_a_k.shape[1]
    past = page_table.shape[1] * LANES
    n_s = bd * n_new
    assert cache_a_k.shape[2] == LANES and n_new <= SUBLANES

    lw = [_layer_weights(l, w_in, w_out, conv_w, subln, lam_q1, lam_k1, lam_q2, lam_k2) for l in range(depth)]
    ffn = (ffn_w1[0].astype(BF16), ffn_w3[0].astype(BF16), ffn_w2[0].astype(BF16))
    moe = (jnp.pad(moe_router[0], ((0, 0), (0, LANES - N_EXPERTS))),
           jnp.pad(moe_router_b[0], (0, LANES - N_EXPERTS), constant_values=NEG).reshape(1, LANES),
           moe_w1[0].astype(BF16), moe_w3[0].astype(BF16), moe_w2[0].astype(BF16))
    row = lambda v: v.reshape(1, -1)
    tiles_a, tiles_b = _prompt_tiles(rel_bias)
    sbias_a, sbias_b = _sample_bias(rel_bias, past, n_new)
    lp = past + LANES
    sbias_a = sbias_a.reshape(N_HEADS_A * SUBLANES, lp)
    sbias_b = jnp.repeat(sbias_b, 2, axis=0).reshape(2 * N_HEADS_B * SUBLANES, lp)
    feat_major = lambda c: jnp.moveaxis(c, 2, -1).reshape(depth, n_pool, -1, LANES)
    ca_k, ca_v, ca_i, cb_k, cb_v = (feat_major(c) for c in (cache_a_k, cache_a_v, cache_a_kidx, cache_b_k, cache_b_v))

    def ffn_block(l, x):
        if l == 0:
            return _ffn(x, row(norm_ffn[l]), *ffn)
        return _moe_final(x, row(norm_ffn[l]), *moe, row(norm_final))

    def rows_of(pr, b, t):
        def tok_major(a, *s):
            c = a.shape[1]
            return jnp.moveaxis(jnp.moveaxis(a, 1, 0).reshape(c, b, t), 0, -1).reshape(b, t, *s)
        return (tok_major(pr["kat"], N_HEADS_A, HEAD_DIM), tok_major(pr["vat"], N_HEADS_A, HEAD_DIM),
                tok_major(pr["kwt"][:, :IDX_DIM], IDX_DIM), tok_major(pr["kbt"], N_HEADS_B, HEAD_DIM),
                tok_major(pr["vbt"], N_HEADS_B, HEAD_DIM),
                pr["u"].reshape(b, t, D_C)[:, t - (CONV_WIDTH - 1):])

    x = x_prompt.reshape(bsz * seq, D_MODEL)
    rows_p = []
    zeros_state = jnp.zeros((bsz, SUBLANES, D_C), F32)
    for l in range(depth):
        pr = _proj(x, row(norm_mix[l]), lw[l]["w_tok"], lw[l]["w_feat"], bsz, seq)
        rows_p.append(rows_of(pr, bsz, seq))
        oa = _dsa_prompt(pr, tiles_a, bsz, seq)
        ob = _diff_prompt(pr, tiles_b, lw[l]["lam"], lw[l]["g_col"], lw[l]["out_scale"], bsz, seq)
        x = _outproj(x, oa, ob, pr["u"], pr["gb"], lw[l]["cw"], lw[l]["w_out"], pr["u"], zeros_state,
                     seq, False)
        x = ffn_block(l, x)
    y_prompt = x.reshape(bsz, seq, D_MODEL)

    x = x_sample.reshape(n_s, D_MODEL)
    rows_s = []
    for l in range(depth):
        pr = _proj(x, row(norm_mix[l]), lw[l]["w_tok"], lw[l]["w_feat"], 1, n_s)
        rows_s.append(rows_of(pr, bd, n_new))
        r3 = lambda a: a.reshape(bd, n_new, a.shape[-1])
        new_rows = lambda a: _pad_rows(r3(a), LANES)
        new_rows_t = lambda a: _pad_rows(jnp.moveaxis(a[0].reshape(-1, bd, n_new), 0, -1), LANES).astype(BF16)
        qi_s = _pad_rows(r3(pr["qi"]).reshape(bd, n_new, IDX_HEADS, IDX_DIM).transpose(0, 2, 1, 3),
                         SUBLANES).reshape(bd, IDX_HEADS * SUBLANES, IDX_DIM)
        w_tok = jnp.moveaxis(pr["kwt"][0, IDX_DIM:IDX_DIM + IDX_HEADS].reshape(IDX_HEADS, bd, n_new), 0, -1)
        w_s = jnp.pad(_pad_rows(w_tok, SUBLANES) * IDX_HEADS ** -0.5, ((0, 0), (0, 0), (0, LANES - IDX_HEADS)))
        mb = _idx_sample(page_table, qi_s, w_s, new_rows(pr["kib"]), ca_i, l, past, n_new, N_HEADS_A)
        oa = _attn_sample(page_table, _block_diag_queries(r3(pr["qa"]), 1), mb, sbias_a,
                          new_rows(pr["kab"]), new_rows_t(pr["vatb"]), ca_k, ca_v, l, past)
        ob = _attn_sample(page_table, _block_diag_queries(r3(pr["qb"]), 2), None, sbias_b,
                          new_rows(pr["kbb"]), new_rows_t(pr["vbtb"]), cb_k, cb_v, l, past,
                          lam=lw[l]["lam"], g_row=lw[l]["g_row"], diff_scale=lw[l]["out_scale"])
        flat = lambda o: o[:, :n_new].reshape(n_s, o.shape[-1]).astype(BF16)
        st = state_conv[l]
        zero = jnp.zeros((bd, n_new - 1, D_C), F32)
        a = jnp.concatenate([st[:, 1:2], zero], axis=1).reshape(n_s, D_C)
        b = jnp.concatenate([st, zero[:, 1:]], axis=1).reshape(n_s, D_C)
        x = _outproj(x, flat(oa), flat(ob), pr["u"], pr["gb"], lw[l]["cw"], lw[l]["w_out"], a, b,
                     n_new, True)
        x = ffn_block(l, x)
    y_sample = x.reshape(bd, n_new, D_MODEL)

    stack = lambda rows: tuple(jnp.stack([r[i] for r in rows]) for i in range(6))
    return (y_prompt, y_sample) + stack(rows_p) + stack(rows_s)
```

```python
import functools
import math

import jax
import jax.numpy as jnp
from jax import lax
from jax.experimental import pallas as pl
from jax.experimental.pallas import tpu as pltpu

F32 = jnp.float32
BF16 = jnp.bfloat16
I32 = jnp.int32

D_MODEL = 1024
HEAD_DIM = 64
N_HEADS_A = 6
N_HEADS_B = 6
D_A = N_HEADS_A * HEAD_DIM
D_B = N_HEADS_B * HEAD_DIM
D_C = D_MODEL - D_A - D_B
DIFF_DIM = HEAD_DIM // 2
IDX_HEADS = 4
IDX_DIM = 64
TOPK_MAX = 256
CONV_WIDTH = 3
NUM_BUCKETS = 32
MAX_DISTANCE = 128
N_EXPERTS = 8
EPS = 1e-6

LANES = 128
SUBLANES = 8
VMEM_LIMIT = 56 * 1024 * 1024
LOG2E = 1.4426950408889634
NEG = -1e30
INT_MIN = -2**31
BIG_IDX = 2**30
I16 = jnp.int16
MIN16 = -2**15
BIG16 = 2**15 - 1
PACK16 = 16

ATT_TILE = 256
PAGES_PER_STEP = 32
N_PAIRS = D_A // LANES

R_QA, R_QI, R_QB, R_KA, R_KB, R_KI, R_HC, R_GB, R_GC, R_END = (
    0, 384, 640, 1024, 1408, 1792, 1920, 2176, 2432, 2688)
T_KA, T_VA, T_KB, T_VB, T_KW, T_END = 0, 384, 768, 1152, 1536, 1664


def _cparams(sem):
    return pltpu.CompilerParams(dimension_semantics=sem, vmem_limit_bytes=VMEM_LIMIT)


def _dot_t(a, b):
    return lax.dot_general(a, b, (((1,), (1,)), ((), ())), preferred_element_type=F32)


def _dot(a, b):
    return jnp.dot(a, b, preferred_element_type=F32)


def _rms(x, g):
    ms = jnp.mean(x * x, axis=-1, keepdims=True)
    return x * lax.rsqrt(ms + EPS) * g


def _proj_kernel(x_ref, g_ref, w_ref, wt_ref, qa_ref, qi_ref, qb_ref, kab_ref, kbb_ref, kib_ref,
                 u_ref, gb_ref, kat_ref, vat_ref, vatb_ref, kbt_ref, vbt_ref, vbtb_ref, kwt_ref):
    h = _rms(x_ref[...], g_ref[...]).astype(BF16)

    def mm(a, b):
        return _dot(h, w_ref[:, a:b])

    def mt(a, b):
        return _dot_t(wt_ref[a:b, :], h)

    qa_ref[...] = (mm(R_QA, R_QI) * (HEAD_DIM ** -0.5 * LOG2E)).astype(BF16)
    qi_ref[...] = (mm(R_QI, R_QB) * IDX_DIM ** -0.5).astype(BF16)
    qb_ref[...] = (mm(R_QB, R_KA) * (DIFF_DIM ** -0.5 * LOG2E)).astype(BF16)
    kab_ref[...] = mm(R_KA, R_KB).astype(BF16)
    kbb_ref[...] = mm(R_KB, R_KI).astype(BF16)
    kib_ref[...] = mm(R_KI, R_HC)[:, :IDX_DIM].astype(BF16)
    hc = mm(R_HC, R_GB)
    gb_ref[...] = mm(R_GB, R_GC)
    u_ref[...] = mm(R_GC, R_END) * hc
    kat_ref[...] = mt(T_KA, T_VA)
    va = mt(T_VA, T_KB)
    vat_ref[...] = va
    vatb_ref[...] = va.astype(BF16)
    kbt_ref[...] = mt(T_KB, T_VB)
    vb = mt(T_VB, T_KW)
    vbt_ref[...] = vb
    vbtb_ref[...] = vb.astype(BF16)
    kwt_ref[...] = mt(T_KW, T_END)


def _proj(x, g, w, wt, bsz, t):
    tm = min(ATT_TILE, t)
    assert t % tm == 0
    tpb = t // tm
    tok = dict(qa=(D_A, BF16), qi=(IDX_HEADS * IDX_DIM, BF16), qb=(D_B, BF16), kab=(D_A, BF16),
               kbb=(D_B, BF16), kib=(IDX_DIM, BF16), u=(D_C, F32), gb=(D_C, F32))
    feat = dict(kat=(D_A, F32), vat=(D_A, F32), vatb=(D_A, BF16), kbt=(D_B, F32), vbt=(D_B, F32),
                vbtb=(D_B, BF16), kwt=(LANES, F32))
    outs = pl.pallas_call(
        _proj_kernel,
        grid=(bsz * tpb,),
        in_specs=[pl.BlockSpec((tm, D_MODEL), lambda i: (i, 0)),
                  pl.BlockSpec((1, D_MODEL), lambda i: (0, 0)),
                  pl.BlockSpec((D_MODEL, R_END), lambda i: (0, 0)),
                  pl.BlockSpec((T_END, D_MODEL), lambda i: (0, 0))],
        out_specs=[pl.BlockSpec((tm, c), lambda i: (i, 0)) for c, _ in tok.values()]
        + [pl.BlockSpec((None, c, tm), lambda i: (i // tpb, 0, i % tpb)) for c, _ in feat.values()],
        out_shape=[jax.ShapeDtypeStruct((bsz * t, c), d) for c, d in tok.values()]
        + [jax.ShapeDtypeStruct((bsz, c, t), d) for c, d in feat.values()],
        compiler_params=_cparams(("parallel",)),
        name="proj",
    )(x, g, w, wt)
    return dict(zip(list(tok) + list(feat), outs))


def _sortable_key(score):
    score = jnp.where(score == 0.0, 0.0, score)
    bits = pltpu.bitcast(score, I32)
    return bits ^ ((bits >> 31) & 0x7FFFFFFF)


def _topk_search(count, transform, shape, topk, idx_bits):
    kf = float(topk)
    t = jnp.where(count(lambda blk, c: blk >= 0, None) >= kf, 0, INT_MIN).astype(I32)

    def value_bit(i, t):
        cand = t | lax.shift_left(jnp.int32(1), 30 - i)
        return jnp.where(count(lambda blk, c: blk >= c, cand) >= kf, cand, t)

    t = lax.fori_loop(0, 31, value_bit, t)
    transform(jnp.maximum(t, INT_MIN + 1))

    def index_bit(i, m):
        cand = m | lax.shift_left(jnp.int32(1), idx_bits - 1 - i)
        return jnp.where(count(lambda blk, c: blk < c, cand) < kf, cand, m)

    return lax.fori_loop(0, idx_bits, index_bit, jnp.zeros(shape, I32))


def _rewrite_keys(blk, t, idx):
    return jnp.where(blk > t, -1, jnp.where(blk == t, idx, BIG_IDX))


def _tree_sum(parts):
    while len(parts) > 1:
        parts = [a + b for a, b in zip(parts[::2], parts[1::2])] + parts[len(parts) & ~1:]
    return parts[0]


def _topk_search_16(hi_ref, lo_ref, n_ch, tk, topk, idx_bits):
    tq = hi_ref.shape[1]
    one, zero = jnp.int16(1), jnp.int16(0)
    kf = float(topk)
    row = lax.broadcasted_iota(I32, (tk, tq), 0)

    def chunk(kc):
        return pl.ds(pl.multiple_of(kc * tk, tk), tk)

    def count(ref, pred, cand):
        c16 = jnp.broadcast_to(cand, (PACK16, tq)).astype(I16)

        def body(i, acc):
            blks = [ref[chunk(2 * i + s), :] for s in range(2)]
            return acc + _tree_sum([jnp.where(pred(blk[g * PACK16:(g + 1) * PACK16], c16), one, zero)
                                    for blk in blks for g in range(tk // PACK16)])

        acc = lax.fori_loop(0, n_ch // 2, body, jnp.zeros((PACK16, tq), I16))
        return jnp.sum(acc.astype(F32), axis=0, keepdims=True)

    def largest_with(ref, need):
        ge = lambda c: count(ref, lambda b, c16: b >= c16, c)
        t = jnp.where(ge(jnp.zeros((1, tq), I32)) >= need, 0, MIN16).astype(I32)
        return lax.fori_loop(
            0, 15, lambda i, t: jnp.where(ge(t | lax.shift_left(jnp.int32(1), 14 - i)) >= need,
                                          t | lax.shift_left(jnp.int32(1), 14 - i), t), t)

    def full16(v):
        return jnp.broadcast_to(v, (tk, tq)).astype(I16)

    t_hi = jnp.maximum(largest_with(hi_ref, kf), MIN16 + 1)
    need = kf - count(hi_ref, lambda b, c16: b > c16, t_hi)
    thi_f = full16(t_hi)

    def keep_candidates(kc, c):
        lo_ref[chunk(kc), :] = jnp.where(hi_ref[chunk(kc), :] == thi_f, lo_ref[chunk(kc), :], jnp.int16(MIN16))
        return c

    lax.fori_loop(0, n_ch, keep_candidates, 0)
    tlo_f = full16(largest_with(lo_ref, need))
    minus1, big = jnp.int16(-1), jnp.int16(BIG16)

    def rewrite(kc, c):
        hi = hi_ref[chunk(kc), :]
        lo = lo_ref[chunk(kc), :]
        idx = (kc * tk + row).astype(I16)
        inner = jnp.where(lo > tlo_f, minus1, jnp.where(lo == tlo_f, idx, big))
        hi_ref[chunk(kc), :] = jnp.where(hi > thi_f, minus1, jnp.where(hi == thi_f, inner, big))
        return c

    lax.fori_loop(0, n_ch, rewrite, 0)

    def index_bit(i, m):
        cand = m | lax.shift_left(jnp.int32(1), idx_bits - 1 - i)
        return jnp.where(count(hi_ref, lambda b, c16: b < c16, cand) < kf, cand, m)

    return lax.fori_loop(0, idx_bits, index_bit, jnp.zeros((1, tq), I32))


N_TILE_VARIANTS = 4


def _tile_variant(kc, qi):
    return jnp.where(kc == qi, 0, jnp.where(kc == qi - 1, 1, jnp.where(kc > qi, 3, 2)))


def _softmax_step_t(lg_of, vt, m_ref, l_ref, acc_ref):
    alphas, probs = [], []
    for j in range(m_ref.shape[-1] // LANES):
        cols = slice(j * LANES, (j + 1) * LANES)
        lg = lg_of(cols)
        m_old = m_ref[:, cols]
        m_new = jnp.maximum(m_old, jnp.max(lg, axis=0, keepdims=True))
        alpha = jnp.exp2(m_old - m_new)
        p = jnp.exp2(lg - m_new)
        l_ref[:, cols] = alpha * l_ref[:, cols] + jnp.sum(p, axis=0, keepdims=True)
        m_ref[:, cols] = m_new
        alphas.append(alpha)
        probs.append(p.astype(BF16))
    acc_ref[...] = (jnp.concatenate(alphas, axis=1) * acc_ref[...]
                    + _dot(vt, jnp.concatenate(probs, axis=1)))


def _dsa_prompt_kernel(qi_ref, kwt_ref, kib_ref, qa_ref, kab_ref, vat_ref, tiles_ref, oa_ref,
                       hi_sc, lo_sc, mb_sc, lg_sc, qm_sc, m_sc, l_sc, acc_sc, *, topk, idx_bits):
    tq = ATT_TILE
    qi = pl.program_id(1)
    n_ch = qi + 1
    q0 = qi * tq
    row = lax.broadcasted_iota(I32, (tq, tq), 0)
    col = lax.broadcasted_iota(I32, (tq, tq), 1)

    w4 = kwt_ref[IDX_DIM:IDX_DIM + SUBLANES, :] * IDX_HEADS ** -0.5

    n_trips = (n_ch + 1) // 2

    def score_chunks(i, c):
        for s in range(2):
            kc = 2 * i + s
            k0 = pl.multiple_of(kc * tq, tq)
            kch = kib_ref[pl.ds(pl.multiple_of(jnp.minimum(kc, pl.num_programs(1) - 1) * tq, tq), tq), :]
            sc = jnp.zeros((tq, tq), F32)
            for h in range(IDX_HEADS):
                sc = sc + w4[h:h + 1, :] * jnp.maximum(_dot_t(kch, qi_ref[:, h * IDX_DIM:(h + 1) * IDX_DIM]), 0.0)
            valid = (k0 + row) <= (q0 + col)
            key = jnp.where(valid, _sortable_key(sc), INT_MIN)
            hi_sc[pl.ds(k0, tq), :] = (key >> 16).astype(I16)
            lo_sc[pl.ds(k0, tq), :] = ((key & 0xFFFF) + MIN16).astype(I16)
        return c

    lax.fori_loop(0, n_trips, score_chunks, 0)

    thr = _topk_search_16(hi_sc, lo_sc, 2 * n_trips, tq, topk, idx_bits)

    lane = lax.broadcasted_iota(I32, (tq, LANES), 1)
    for p in range(N_PAIRS):
        qp = qa_ref[:, p * LANES:(p + 1) * LANES]
        qm_sc[p, 0:tq, :] = jnp.where(lane < HEAD_DIM, qp, jnp.zeros_like(qp))
        qm_sc[p, tq:2 * tq, :] = jnp.where(lane >= HEAD_DIM, qp, jnp.zeros_like(qp))
    m_sc[...] = jnp.full(m_sc.shape, NEG, F32)
    l_sc[...] = jnp.zeros(l_sc.shape, F32)
    acc_sc[...] = jnp.zeros(acc_sc.shape, F32)

    def logits_into(slot, kc):
        k0 = pl.multiple_of(kc * tq, tq)
        for p in range(N_PAIRS):
            kp = kab_ref[pl.ds(k0, tq), p * LANES:(p + 1) * LANES]
            lg_sc[slot, :, 2 * p * tq:2 * (p + 1) * tq] = _dot_t(kp, qm_sc[p])

    last = pl.num_programs(1) - 1
    logits_into(0, 0)
    logits_into(1, jnp.minimum(1, last))

    def attend(i, c, near):
        for s in range(2):
            kc = 2 * i + s
            k0 = pl.multiple_of(jnp.minimum(kc, last) * tq, tq)
            var = _tile_variant(kc, qi)
            mb_sc[s] = jnp.where(hi_sc[pl.ds(k0, tq), :].astype(I32) <= thr, 0.0, NEG)
            for h in range(N_HEADS_A):
                p = h // 2
                vt = vat_ref[p * LANES:(p + 1) * LANES, pl.ds(k0, tq)]

                def lg_of(cols, h=h):
                    shifted = slice(h * tq + cols.start, h * tq + cols.stop)
                    lg = lg_sc[s, :, shifted] + mb_sc[s, :, cols]
                    return lg + tiles_ref[var, h, :, cols] if near else lg

                _softmax_step_t(lg_of, vt, m_sc.at[h], l_sc.at[h], acc_sc.at[h])
            logits_into(s, jnp.minimum(kc + 2, last))
        return c

    n_far = jnp.maximum(qi - 1, 0) // 2
    lax.fori_loop(0, n_far, functools.partial(attend, near=False), 0)
    lax.fori_loop(n_far, n_trips, functools.partial(attend, near=True), 0)

    srow = lax.broadcasted_iota(I32, (LANES, tq), 0)
    for p in range(N_PAIRS):
        o_lo = acc_sc[2 * p] / l_sc[2 * p]
        o_hi = acc_sc[2 * p + 1] / l_sc[2 * p + 1]
        o_t = jnp.where(srow < HEAD_DIM, o_lo, o_hi)
        oa_ref[:, p * LANES:(p + 1) * LANES] = o_t.T.astype(BF16)


def _dsa_prompt(pr, tiles, bsz, t):
    tq = ATT_TILE
    assert t % tq == 0
    topk = min(TOPK_MAX, t // 4)
    idx_bits = max(1, (t - 1).bit_length())
    r3 = lambda a: a.reshape(bsz, t, a.shape[-1])
    qtile = lambda c: pl.BlockSpec((None, tq, c), lambda b, i: (b, i, 0))
    full = lambda c: pl.BlockSpec((None, t, c), lambda b, i: (b, 0, 0))
    out = pl.pallas_call(
        functools.partial(_dsa_prompt_kernel, topk=topk, idx_bits=idx_bits),
        grid=(bsz, t // tq),
        in_specs=[qtile(IDX_HEADS * IDX_DIM),
                  pl.BlockSpec((None, LANES, tq), lambda b, i: (b, 0, i)),
                  full(IDX_DIM), qtile(D_A), full(D_A),
                  pl.BlockSpec((None, D_A, t), lambda b, i: (b, 0, 0)),
                  pl.BlockSpec((N_TILE_VARIANTS, N_HEADS_A, tq, tq), lambda b, i: (0, 0, 0, 0))],
        out_specs=qtile(D_A),
        out_shape=jax.ShapeDtypeStruct((bsz, t, D_A), BF16),
        scratch_shapes=[pltpu.VMEM((t + tq, tq), I16), pltpu.VMEM((t + tq, tq), I16),
                        pltpu.VMEM((2, tq, tq), F32),
                        pltpu.VMEM((2, tq, N_HEADS_A * tq), F32),
                        pltpu.VMEM((N_PAIRS, 2 * tq, LANES), BF16),
                        pltpu.VMEM((N_HEADS_A, 1, tq), F32), pltpu.VMEM((N_HEADS_A, 1, tq), F32),
                        pltpu.VMEM((N_HEADS_A, LANES, tq), F32)],
        compiler_params=_cparams(("parallel", "arbitrary")),
        name="dsa_prompt",
    )(r3(pr["qi"]), pr["kwt"], r3(pr["kib"]), r3(pr["qa"]), r3(pr["kab"]), pr["vatb"], tiles)
    return out.reshape(bsz * t, D_A)


def _diff_prompt_kernel(lam_ref, qb_ref, kbb_ref, vbt_ref, tiles_ref, g_ref, ob_ref,
                        lg_sc, qm_sc, m_sc, l_sc, acc_sc, *, out_scale):
    tq = ATT_TILE
    qi = pl.program_id(1)
    lane = lax.broadcasted_iota(I32, (tq, LANES), 1)
    for p in range(N_PAIRS):
        qp = qb_ref[:, p * LANES:(p + 1) * LANES]
        for v in range(4):
            inside = (lane >= v * DIFF_DIM) & (lane < (v + 1) * DIFF_DIM)
            qm_sc[p, v * tq:(v + 1) * tq, :] = jnp.where(inside, qp, jnp.zeros_like(qp))
    m_sc[...] = jnp.full(m_sc.shape, NEG, F32)
    l_sc[...] = jnp.zeros(l_sc.shape, F32)
    acc_sc[...] = jnp.zeros(acc_sc.shape, F32)

    n_ch = qi + 1

    def logits_into(slot, kc):
        k0 = pl.multiple_of(kc * tq, tq)
        for p in range(N_PAIRS):
            kp = kbb_ref[pl.ds(k0, tq), p * LANES:(p + 1) * LANES]
            for hh in range(2):
                u0 = (4 * p + 2 * hh) * tq
                lg_sc[slot, :, u0:u0 + 2 * tq] = _dot_t(kp, qm_sc[p, 2 * hh * tq:2 * (hh + 1) * tq, :])

    last = pl.num_programs(1) - 1
    logits_into(0, 0)
    logits_into(1, jnp.minimum(1, last))

    def attend(i, c, near):
        for s in range(2):
            kc = 2 * i + s
            k0 = pl.multiple_of(jnp.minimum(kc, last) * tq, tq)
            var = _tile_variant(kc, qi)
            for p in range(N_PAIRS):
                vt = vbt_ref[p * LANES:(p + 1) * LANES, pl.ds(k0, tq)]
                for v in range(4):
                    def lg_of(cols, u=4 * p + v, h=2 * p + v // 2):
                        shifted = slice(u * tq + cols.start, u * tq + cols.stop)
                        lg = lg_sc[s, :, shifted]
                        return lg + tiles_ref[var, h, :, cols] if near else lg

                    _softmax_step_t(lg_of, vt, m_sc.at[p, v], l_sc.at[p, v], acc_sc.at[p, v])
            logits_into(s, jnp.minimum(kc + 2, last))
        return c

    n_far = jnp.maximum(qi - 1, 0) // 2
    lax.fori_loop(0, n_far, functools.partial(attend, near=False), 0)
    lax.fori_loop(n_far, (n_ch + 1) // 2, functools.partial(attend, near=True), 0)

    lam = lam_ref[0]
    lo = lax.broadcasted_iota(I32, (LANES, tq), 0) < HEAD_DIM
    for p in range(N_PAIRS):
        def head(v1, v2):
            return acc_sc[p, v1] / l_sc[p, v1] - lam * (acc_sc[p, v2] / l_sc[p, v2])

        o = jnp.where(lo, head(0, 1), head(2, 3))
        sq = o * o
        ms_lo = jnp.sum(jnp.where(lo, sq, 0.0), axis=0, keepdims=True) * (1.0 / HEAD_DIM)
        ms_hi = jnp.sum(jnp.where(lo, 0.0, sq), axis=0, keepdims=True) * (1.0 / HEAD_DIM)
        r = jnp.where(lo, lax.rsqrt(ms_lo + EPS), lax.rsqrt(ms_hi + EPS))
        y = o * r * g_ref[...] * out_scale
        ob_ref[:, p * LANES:(p + 1) * LANES] = y.T.astype(BF16)


def _diff_prompt(pr, tiles, lam, g_col, out_scale, bsz, t):
    tq = ATT_TILE
    r3 = lambda a: a.reshape(bsz, t, a.shape[-1])
    qtile = lambda c: pl.BlockSpec((None, tq, c), lambda b, i: (b, i, 0))
    full = lambda c: pl.BlockSpec((None, t, c), lambda b, i: (b, 0, 0))
    out = pl.pallas_call(
        functools.partial(_diff_prompt_kernel, out_scale=out_scale),
        grid=(bsz, t // tq),
        in_specs=[pl.BlockSpec(memory_space=pltpu.SMEM), qtile(D_B), full(D_B),
                  pl.BlockSpec((None, D_B, t), lambda b, i: (b, 0, 0)),
                  pl.BlockSpec((N_TILE_VARIANTS, N_HEADS_B, tq, tq), lambda b, i: (0, 0, 0, 0)),
                  pl.BlockSpec((LANES, 1), lambda b, i: (0, 0))],
        out_specs=qtile(D_B),
        out_shape=jax.ShapeDtypeStruct((bsz, t, D_B), BF16),
        scratch_shapes=[pltpu.VMEM((2, tq, 2 * N_HEADS_B * tq), F32),
                        pltpu.VMEM((N_PAIRS, 4 * tq, LANES), BF16),
                        pltpu.VMEM((N_PAIRS, 4, 1, tq), F32), pltpu.VMEM((N_PAIRS, 4, 1, tq), F32),
                        pltpu.VMEM((N_PAIRS, 4, LANES, tq), F32)],
        compiler_params=_cparams(("parallel", "arbitrary")),
        name="diff_prompt",
    )(lam, r3(pr["qb"]), r3(pr["kbb"]), pr["vbtb"], tiles, g_col)
    return out.reshape(bsz * t, D_B)


def _gather_chunk(page_refs):
    return jnp.concatenate([r[...] for r in page_refs], axis=1).astype(BF16)


def _idx_sample_kernel(pt_ref, qi_ref, w_ref, kin_ref, *rest, past, topk, idx_bits, n_rep):
    del pt_ref
    pages = rest[:PAGES_PER_STEP]
    mb_ref, s_sc = rest[PAGES_PER_STEP:]
    g = pl.program_id(1)
    ck = PAGES_PER_STEP * LANES
    w4 = w_ref[...]

    def scores(s_all):
        sc = jnp.zeros((SUBLANES, s_all.shape[1]), F32)
        for h in range(IDX_HEADS):
            sc = sc + w4[:, h:h + 1] * jnp.maximum(s_all[h * SUBLANES:(h + 1) * SUBLANES], 0.0)
        return _sortable_key(sc)

    s_sc[:, pl.ds(pl.multiple_of(g * ck, ck), ck)] = scores(_dot(qi_ref[...], _gather_chunk(pages)))

    @pl.when(g == pl.num_programs(1) - 1)
    def _():
        row = lax.broadcasted_iota(I32, (SUBLANES, LANES), 0)
        col = lax.broadcasted_iota(I32, (SUBLANES, LANES), 1)
        new = scores(_dot_t(qi_ref[...], kin_ref[...]))
        s_sc[:, past:past + LANES] = jnp.where(col <= row, new, INT_MIN)
        n_groups = past // LANES + 1

        def group(i):
            return slice(i * LANES, (i + 1) * LANES)

        def count(pred, cand):
            cb = None if cand is None else jnp.broadcast_to(cand, (SUBLANES, LANES))
            parts = [jnp.where(pred(s_sc[:, group(i)], cb), 1, 0) for i in range(n_groups)]
            while len(parts) > 1:
                parts = [a + b for a, b in zip(parts[::2], parts[1::2])] + parts[len(parts) & ~1:]
            return jnp.sum(parts[0].astype(F32), axis=1, keepdims=True)

        def transform(t):
            for i in range(n_groups):
                s_sc[:, group(i)] = _rewrite_keys(s_sc[:, group(i)], t, i * LANES + col)

        m = _topk_search(count, transform, (SUBLANES, 1), topk, idx_bits)
        mb = jnp.where(s_sc[...] <= m, 0.0, NEG)
        mb_ref[...] = jnp.concatenate([mb] * n_rep, axis=0)


def _page_specs(width, layer, n):
    return [pl.BlockSpec((None, None, width, LANES),
                         functools.partial(lambda b, g, pt, j: (layer, pt[b, g * PAGES_PER_STEP + j], 0, 0), j=j))
            for j in range(n)]


def _idx_sample(page_table, qi_s, w_s, kin_s, cache_kidx_t, layer, past, n_new, n_rep):
    bd = qi_s.shape[0]
    lp = past + LANES
    n_steps = past // (PAGES_PER_STEP * LANES)
    topk = min(TOPK_MAX, (past + n_new) // 4)
    idx_bits = (past + n_new - 1).bit_length()
    per_b = lambda r, c: pl.BlockSpec((None, r, c), lambda b, g, pt: (b, 0, 0))
    return pl.pallas_call(
        functools.partial(_idx_sample_kernel, past=past, topk=topk, idx_bits=idx_bits, n_rep=n_rep),
        grid_spec=pltpu.PrefetchScalarGridSpec(
            num_scalar_prefetch=1, grid=(bd, n_steps),
            in_specs=[per_b(IDX_HEADS * SUBLANES, IDX_DIM), per_b(SUBLANES, LANES), per_b(LANES, IDX_DIM)]
            + _page_specs(IDX_DIM, layer, PAGES_PER_STEP),
            out_specs=per_b(n_rep * SUBLANES, lp),
            scratch_shapes=[pltpu.VMEM((SUBLANES, lp), I32)]),
        out_shape=jax.ShapeDtypeStruct((bd, n_rep * SUBLANES, lp), F32),
        compiler_params=_cparams(("parallel", "arbitrary")),
        name="idx_sample",
    )(page_table, qi_s, w_s, kin_s, *([cache_kidx_t] * PAGES_PER_STEP))


def _attn_sample_kernel(pt_ref, *refs, past, use_mask, diff_scale):
    del pt_ref
    refs = list(refs)
    lam_ref = refs.pop(0) if diff_scale is not None else None
    q_ref = refs.pop(0)
    mb_ref = refs.pop(0) if use_mask else None
    bias_ref, kn_ref, vn_ref = refs[:3]
    refs = refs[3:]
    g_ref = refs.pop(0) if diff_scale is not None else None
    kpages = refs[:PAGES_PER_STEP]
    vpages = refs[PAGES_PER_STEP:2 * PAGES_PER_STEP]
    o_ref, m_sc, l_sc, acc_sc = refs[2 * PAGES_PER_STEP:]
    g = pl.program_id(1)
    ck = PAGES_PER_STEP * LANES

    @pl.when(g == 0)
    def _():
        m_sc[...] = jnp.full(m_sc.shape, NEG, F32)
        l_sc[...] = jnp.zeros(l_sc.shape, F32)
        acc_sc[...] = jnp.zeros(acc_sc.shape, F32)

    def attend(lg, c0, width, pv):
        lg = lg + bias_ref[:, pl.ds(c0, width)]
        if use_mask:
            lg = lg + mb_ref[:, pl.ds(c0, width)]
        m_old = m_sc[...]
        m_new = jnp.maximum(m_old, jnp.max(lg, axis=1, keepdims=True))
        alpha = jnp.exp2(m_old - m_new)
        p = jnp.exp2(lg - m_new)
        l_sc[...] = alpha * l_sc[...] + jnp.sum(p, axis=1, keepdims=True)
        acc_sc[...] = alpha * acc_sc[...] + pv(p.astype(BF16))
        m_sc[...] = m_new

    q = q_ref[...]
    vt = _gather_chunk(vpages)
    attend(_dot(q, _gather_chunk(kpages)), pl.multiple_of(g * ck, ck), ck, lambda p: _dot_t(p, vt))

    @pl.when(g == pl.num_programs(1) - 1)
    def _():
        attend(_dot_t(q, kn_ref[...]), past, LANES, lambda p: _dot(p, vn_ref[...]))
        head_of_col = lax.broadcasted_iota(I32, (SUBLANES, D_A), 1) // HEAD_DIM
        n_comp = q.shape[0] // (N_HEADS_A * SUBLANES)

        def gather_heads(comp):
            out = jnp.zeros((SUBLANES, D_A), F32)
            for h in range(N_HEADS_A):
                r0 = (h * n_comp + comp) * SUBLANES
                o_h = acc_sc[r0:r0 + SUBLANES, :] / l_sc[r0:r0 + SUBLANES, :]
                out = jnp.where(head_of_col == h, o_h, out)
            return out

        if diff_scale is None:
            o_ref[...] = gather_heads(0)
        else:
            o = gather_heads(0) - lam_ref[0] * gather_heads(1)
            sq = o * o
            r = jnp.zeros((SUBLANES, D_A), F32)
            for h in range(N_HEADS_B):
                ms = jnp.sum(jnp.where(head_of_col == h, sq, 0.0), axis=1, keepdims=True) * (1.0 / HEAD_DIM)
                r = jnp.where(head_of_col == h, lax.rsqrt(ms + EPS), r)
            o_ref[...] = o * r * g_ref[...] * diff_scale


def _attn_sample(page_table, q_bd, mb, bias_s, kn_s, vn_s, cache_kt, cache_vt, layer, past,
                 lam=None, g_row=None, diff_scale=None):
    bd, rows, _ = q_bd.shape
    lp = past + LANES
    n_steps = past // (PAGES_PER_STEP * LANES)
    use_mask = mb is not None
    per_b = lambda *s: pl.BlockSpec((None,) + s, lambda b, g, pt: (b,) + (0,) * len(s))
    const = lambda *s: pl.BlockSpec(s, lambda b, g, pt: (0,) * len(s))
    args, specs = [], []
    if diff_scale is not None:
        args.append(lam)
        specs.append(pl.BlockSpec(memory_space=pltpu.SMEM))
    args.append(q_bd)
    specs.append(per_b(rows, D_A))
    if use_mask:
        args.append(mb)
        specs.append(per_b(rows, lp))
    args += [bias_s, kn_s, vn_s]
    specs += [const(rows, lp), per_b(LANES, D_A), per_b(LANES, D_A)]
    if diff_scale is not None:
        args.append(g_row)
        specs.append(const(1, D_A))
    args += [cache_kt] * PAGES_PER_STEP + [cache_vt] * PAGES_PER_STEP
    specs += _page_specs(D_A, layer, PAGES_PER_STEP) + _page_specs(D_A, layer, PAGES_PER_STEP)
    return pl.pallas_call(
        functools.partial(_attn_sample_kernel, past=past, use_mask=use_mask, diff_scale=diff_scale),
        grid_spec=pltpu.PrefetchScalarGridSpec(
            num_scalar_prefetch=1, grid=(bd, n_steps), in_specs=specs,
            out_specs=per_b(SUBLANES, D_A),
            scratch_shapes=[pltpu.VMEM((rows, 1), F32), pltpu.VMEM((rows, 1), F32),
                            pltpu.VMEM((rows, D_A), F32)]),
        out_shape=jax.ShapeDtypeStruct((bd, SUBLANES, D_A), F32),
        compiler_params=_cparams(("parallel", "arbitrary")),
        name="attn_sample_diff" if diff_scale is not None else "attn_sample_dsa",
    )(page_table, *args)


def _outproj_kernel(x_ref, oa_ref, ob_ref, u_ref, a_ref, b_ref, gb_ref, cw_ref, wo_ref, o_ref,
                    *, seq, per_row_state):
    tm = x_ref.shape[0]
    u = u_ref[...]
    row = lax.broadcasted_iota(I32, (tm, D_C), 0)
    um1 = pltpu.roll(u, 1, axis=0)
    um2 = pltpu.roll(u, 2, axis=0)
    if per_row_state:
        t = row % seq
        um1 = jnp.where(t == 0, a_ref[...], um1)
        um2 = jnp.where(t < 2, b_ref[...], um2)
    else:
        first = (pl.program_id(0) % (seq // tm)) == 0
        prev = jnp.where(first, b_ref[...], a_ref[...])
        p6 = prev[SUBLANES - 2:SUBLANES - 1, :]
        p7 = prev[SUBLANES - 1:SUBLANES, :]
        um1 = jnp.where(row == 0, p7, um1)
        um2 = jnp.where(row == 0, p6, jnp.where(row == 1, p7, um2))
    y = cw_ref[0:1, :] * um2 + cw_ref[1:2, :] * um1 + cw_ref[2:3, :] * u
    oc = (gb_ref[...] * y).astype(BF16)
    mixed = (_dot(oa_ref[...], wo_ref[0:D_A, :]) + _dot(ob_ref[...], wo_ref[D_A:D_A + D_B, :])
             + _dot(oc, wo_ref[D_A + D_B:D_MODEL, :]))
    o_ref[...] = x_ref[...] + mixed


def _outproj(x, oa, ob, u, gb, cw, wo, a, b, seq, per_row_state):
    n = x.shape[0]
    tm = min(512, n)
    assert n % tm == 0
    rowt = lambda c: pl.BlockSpec((tm, c), lambda i: (i, 0))
    if per_row_state:
        a_spec, b_spec = rowt(D_C), rowt(D_C)
    else:
        assert seq % tm == 0
        a_spec = pl.BlockSpec((SUBLANES, D_C), lambda i: (jnp.maximum(i * (tm // SUBLANES) - 1, 0), 0))
        b_spec = pl.BlockSpec((None, SUBLANES, D_C), lambda i: (i // (seq // tm), 0, 0))
    return pl.pallas_call(
        functools.partial(_outproj_kernel, seq=seq, per_row_state=per_row_state),
        grid=(n // tm,),
        in_specs=[rowt(D_MODEL), rowt(D_A), rowt(D_B), rowt(D_C), a_spec, b_spec, rowt(D_C),
                  pl.BlockSpec((SUBLANES, D_C), lambda i: (0, 0)),
                  pl.BlockSpec((D_MODEL, D_MODEL), lambda i: (0, 0))],
        out_specs=rowt(D_MODEL),
        out_shape=jax.ShapeDtypeStruct((n, D_MODEL), F32),
        compiler_params=_cparams(("parallel",)),
        name="outproj",
    )(x, oa, ob, u, a, b, gb, cw, wo)


def _silu(a):
    return a / (1.0 + jnp.exp(-a))


def _ffn_kernel(x_ref, g_ref, w1_ref, w3_ref, w2_ref, o_ref, *, n_chunks):
    x = x_ref[...]
    h = _rms(x, g_ref[...]).astype(BF16)
    tf = w1_ref.shape[1] // n_chunks
    out = x
    for c in range(n_chunks):
        a1 = _dot(h, w1_ref[:, c * tf:(c + 1) * tf])
        a3 = _dot(h, w3_ref[:, c * tf:(c + 1) * tf])
        out = out + _dot((_silu(a1) * a3).astype(BF16), w2_ref[c * tf:(c + 1) * tf, :])
    o_ref[...] = out


def _ffn(x, g, w1, w3, w2):
    n = x.shape[0]
    f = w1.shape[1]
    tm = min(256, n)
    n_chunks = 2 if f % (2 * LANES) == 0 else 1
    const = lambda s: pl.BlockSpec(s, lambda i: (0, 0), pipeline_mode=pl.Buffered(1))
    return pl.pallas_call(
        functools.partial(_ffn_kernel, n_chunks=n_chunks),
        grid=(n // tm,),
        in_specs=[pl.BlockSpec((tm, D_MODEL), lambda i: (i, 0)), pl.BlockSpec((1, D_MODEL), lambda i: (0, 0)),
                  const((D_MODEL, f)), const((D_MODEL, f)), const((f, D_MODEL))],
        out_specs=pl.BlockSpec((tm, D_MODEL), lambda i: (i, 0)),
        out_shape=jax.ShapeDtypeStruct((n, D_MODEL), F32),
        compiler_params=_cparams(("parallel",)),
        name="ffn",
    )(x, g, w1, w3, w2)


def _moe_kernel(x_ref, g_ref, wr_ref, br_ref, w1_ref, w3_ref, w2_ref, gf_ref, o_ref,
                h_sc, gate_sc, rank_sc, rank_t_sc, xg_sc, og_sc, acc_sc, *, cs):
    e = pl.program_id(1)
    f = pl.program_id(2)
    tm = x_ref.shape[0]
    lane = lax.broadcasted_iota(I32, (tm, LANES), 1)

    @pl.when((e == 0) & (f == 0))
    def _():
        h = _rms(x_ref[...], g_ref[...])
        h_sc[...] = h.astype(BF16)
        logits = jnp.dot(h, wr_ref[...], preferred_element_type=F32,
                         precision=lax.Precision.HIGHEST) + br_ref[...]
        lane_f = lane.astype(F32)
        top1 = jnp.max(logits, axis=1, keepdims=True)
        i1 = jnp.min(jnp.where(logits == top1, lane_f, float(LANES)), axis=1, keepdims=True)
        rest = jnp.where(lane_f == i1, NEG, logits)
        top2 = jnp.max(rest, axis=1, keepdims=True)
        i2 = jnp.min(jnp.where(rest == top2, lane_f, float(LANES)), axis=1, keepdims=True)
        e2 = jnp.exp(top2 - top1)
        den = 1.0 + e2
        gate_sc[...] = jnp.where(lane_f == i1, 1.0 / den, 0.0) + jnp.where(lane_f == i2, e2 / den, 0.0)
        sel = (lane_f == i1) | (lane_f == i2)
        sel_b = jnp.where(sel, 1.0, 0.0).astype(BF16)
        rb = min(ATT_TILE, tm)
        for r0 in range(0, tm, rb):
            earlier = (lax.broadcasted_iota(I32, (rb, tm), 1)
                       < lax.broadcasted_iota(I32, (rb, tm), 0) + r0)
            rank_sc[r0:r0 + rb, :] = jnp.where(sel[r0:r0 + rb], _dot(jnp.where(earlier, 1.0, 0.0).astype(BF16), sel_b), -1.0)
        rank_t_sc[...] = rank_sc[...].T[0:SUBLANES, :]
        acc_sc[...] = jnp.zeros(acc_sc.shape, F32)

    rank = rank_sc[...]
    n_e = jnp.sum(jnp.where((lane == e) & (rank >= 0.0), 1.0, 0.0)).astype(I32)
    n_c = (n_e + (cs - 1)) // cs

    def slots(c):
        return pl.ds(pl.multiple_of(c * cs, cs), cs)

    @pl.when(f == 0)
    def _():
        rank_row = rank_t_sc[pl.ds(e, 1), :]
        slot = lax.broadcasted_iota(I32, (cs, tm), 0).astype(F32)

        def gather(c, carry):
            pick = jnp.where(rank_row == slot + (c * cs).astype(F32), 1.0, 0.0).astype(BF16)
            xg_sc[slots(c), :] = _dot(pick, h_sc[...]).astype(BF16)
            og_sc[slots(c), :] = jnp.zeros((cs, D_MODEL), F32)
            return carry

        lax.fori_loop(0, n_c, gather, 0)

    def expert_chunk(c, carry):
        xg = xg_sc[slots(c), :]
        act = (_silu(_dot(xg, w1_ref[...])) * _dot(xg, w3_ref[...])).astype(BF16)
        og_sc[slots(c), :] += _dot(act, w2_ref[...])
        return carry

    lax.fori_loop(0, n_c, expert_chunk, 0)

    @pl.when(f == pl.num_programs(2) - 1)
    def _():
        rank_col = jnp.sum(jnp.where(lane == e, rank, 0.0), axis=1, keepdims=True)
        gate_col = jnp.sum(jnp.where(lane == e, gate_sc[...], 0.0), axis=1, keepdims=True)
        slot = lax.broadcasted_iota(I32, (tm, cs), 1).astype(F32)

        def scatter(c, carry):
            place = jnp.where(rank_col == slot + (c * cs).astype(F32), 1.0, 0.0).astype(BF16)
            acc_sc[...] += gate_col * _dot(place, og_sc[slots(c), :].astype(BF16))
            return carry

        lax.fori_loop(0, n_c, scatter, 0)

    @pl.when((e == pl.num_programs(1) - 1) & (f == pl.num_programs(2) - 1))
    def _():
        o_ref[...] = _rms(x_ref[...] + acc_sc[...], gf_ref[...])


def _moe_final(x, g, wr, br, w1, w3, w2, g_final):
    n = x.shape[0]
    n_exp, _, f = w1.shape
    tm = min(1024, n)
    cs = min(3 * LANES, tm)
    cap = pl.cdiv(tm, cs) * cs
    n_chunks = 2 if f % (2 * LANES) == 0 else 1
    tf = f // n_chunks
    tok = lambda c: pl.BlockSpec((tm, c), lambda i, e, j: (i, 0))
    const = lambda r, c: pl.BlockSpec((r, c), lambda i, e, j: (0, 0))
    return pl.pallas_call(
        functools.partial(_moe_kernel, cs=cs),
        grid=(n // tm, n_exp, n_chunks),
        in_specs=[tok(D_MODEL), const(1, D_MODEL), const(D_MODEL, LANES), const(1, LANES),
                  pl.BlockSpec((None, D_MODEL, tf), lambda i, e, j: (e, 0, j)),
                  pl.BlockSpec((None, D_MODEL, tf), lambda i, e, j: (e, 0, j)),
                  pl.BlockSpec((None, tf, D_MODEL), lambda i, e, j: (e, j, 0)),
                  const(1, D_MODEL)],
        out_specs=tok(D_MODEL),
        out_shape=jax.ShapeDtypeStruct((n, D_MODEL), F32),
        scratch_shapes=[pltpu.VMEM((tm, D_MODEL), BF16), pltpu.VMEM((tm, LANES), F32),
                        pltpu.VMEM((tm, LANES), F32), pltpu.VMEM((SUBLANES, tm), F32),
                        pltpu.VMEM((cap, D_MODEL), BF16), pltpu.VMEM((cap, D_MODEL), F32),
                        pltpu.VMEM((tm, D_MODEL), F32)],
        compiler_params=_cparams(("parallel", "arbitrary", "arbitrary")),
        name="moe_final",
    )(x, g, wr, br, w1, w3, w2, g_final)


def _t5_bucket(n):
    max_exact = NUM_BUCKETS // 2
    nf = jnp.maximum(n, max_exact).astype(F32)
    large = max_exact + (jnp.log(nf / max_exact) / math.log(MAX_DISTANCE / max_exact)
                         * (NUM_BUCKETS - max_exact)).astype(I32)
    return jnp.where(n < max_exact, n, jnp.minimum(large, NUM_BUCKETS - 1))


def _bias_of_distance(rel_bias, d):
    buckets = _t5_bucket(jnp.arange(MAX_DISTANCE, dtype=I32))
    tab = (rel_bias * LOG2E).T.reshape((rel_bias.shape[1],) + (1,) * d.ndim + (NUM_BUCKETS,))
    out = jnp.broadcast_to(tab[..., 0], (rel_bias.shape[1],) + d.shape)
    for b in range(1, NUM_BUCKETS):
        out = jnp.where(d >= jnp.sum(buckets < b), tab[..., b], out)
    return jnp.where(d < 0, NEG, out)


def _prompt_tiles(rel_bias):
    i = jnp.arange(ATT_TILE, dtype=I32)
    d = i[None, :] - i[:, None]
    far = _bias_of_distance(rel_bias, jnp.full((1, 1), MAX_DISTANCE, I32))
    tiles = jnp.stack([jnp.where(dv < 0, NEG, _bias_of_distance(rel_bias, dv) - far)
                       for dv in (d, d + ATT_TILE, d + 2 * ATT_TILE, d - ATT_TILE)])
    return tiles[:, :N_HEADS_A], tiles[:, N_HEADS_A:]


def _sample_bias(rel_bias, past, n_new):
    i = jnp.arange(SUBLANES, dtype=I32)[:, None]
    k = jnp.arange(past + LANES, dtype=I32)[None, :]
    b = _bias_of_distance(rel_bias, past + i - k)
    b = jnp.where(k < past + n_new, b, NEG)
    return b[:N_HEADS_A], b[N_HEADS_A:]


def _pad_rows(a, rows):
    pad = [(0, 0)] * a.ndim
    pad[-2] = (0, rows - a.shape[-2])
    return jnp.pad(a, pad)


def _block_diag_queries(q, n_comp):
    bd = q.shape[0]
    qp = _pad_rows(q, SUBLANES)[:, None]
    width = HEAD_DIM // n_comp
    owner = jnp.arange(D_A) // width
    blocks = jnp.arange(N_HEADS_A * n_comp)
    keep = (owner[None, :] == blocks[:, None])[None, :, None, :]
    return jnp.where(keep, qp, jnp.zeros_like(qp)).reshape(bd, N_HEADS_A * n_comp * SUBLANES, D_A)


def _layer_weights(l, w_in, w_out, conv_w, subln, lam_q1, lam_k1, lam_q2, lam_k2):
    w = w_in[l]
    qa, ka, va, qi, kw, qb, kb, vb, hc, gb, gc = (
        w[:, a:b] for a, b in ((0, 384), (384, 768), (768, 1152), (1152, 1408), (1408, 1476),
                               (1476, 1860), (1860, 2244), (2244, 2628), (2628, 2884), (2884, 3140),
                               (3140, 3396)))
    kw = jnp.pad(kw, ((0, 0), (0, LANES - kw.shape[1])))
    w_tok = jnp.concatenate([qa, qi, qb, ka, kb, kw, hc, gb, gc], axis=1).astype(BF16)
    w_feat = jnp.concatenate([ka, va, kb, vb, kw], axis=1).T.astype(BF16)
    lam_init = 0.8 - 0.6 * math.exp(-0.3 * l)
    lam = (jnp.exp(jnp.sum(lam_q1[l] * lam_k1[l])) - jnp.exp(jnp.sum(lam_q2[l] * lam_k2[l])) + lam_init)
    return dict(w_tok=w_tok, w_feat=w_feat, w_out=w_out[l].astype(BF16), cw=_pad_rows(conv_w[l], SUBLANES),
                g_col=jnp.tile(subln[l], 2).reshape(LANES, 1), g_row=jnp.tile(subln[l], N_HEADS_B).reshape(1, D_B),
                lam=lam.reshape(1).astype(F32), out_scale=1.0 - lam_init)


def kernel(x_prompt, x_sample, cache_a_k, cache_a_v, cache_a_kidx, cache_b_k, cache_b_v, state_conv,
           page_table, w_in, w_out, norm_mix, norm_ffn, norm_final, rel_bias, lam_q1, lam_k1, lam_q2,
           lam_k2, subln, conv_w, ffn_w1, ffn_w3, ffn_w2, moe_router, moe_router_b, moe_w1, moe_w3,
           moe_w2):
    depth = w_in.shape[0]
    assert depth == 2, "layer 0 is dense, layer 1 is MoE and is followed by the final norm"
    bsz, seq, _ = x_prompt.shape
    bd, n_new, _ = x_sample.shape
    n_pool = cache_a_k.shape[1]
    past = page_table.shape[1] * LANES
    n_s = bd * n_new
    assert cache_a_k.shape[2] == LANES and n_new <= SUBLANES

    lw = [_layer_weights(l, w_in, w_out, conv_w, subln, lam_q1, lam_k1, lam_q2, lam_k2) for l in range(depth)]
    ffn = (ffn_w1[0].astype(BF16), ffn_w3[0].astype(BF16), ffn_w2[0].astype(BF16))
    moe = (jnp.pad(moe_router[0], ((0, 0), (0, LANES - N_EXPERTS))),
           jnp.pad(moe_router_b[0], (0, LANES - N_EXPERTS), constant_values=NEG).reshape(1, LANES),
           moe_w1[0].astype(BF16), moe_w3[0].astype(BF16), moe_w2[0].astype(BF16))
    row = lambda v: v.reshape(1, -1)
    tiles_a, tiles_b = _prompt_tiles(rel_bias)
    sbias_a, sbias_b = _sample_bias(rel_bias, past, n_new)
    lp = past + LANES
    sbias_a = sbias_a.reshape(N_HEADS_A * SUBLANES, lp)
    sbias_b = jnp.repeat(sbias_b, 2, axis=0).reshape(2 * N_HEADS_B * SUBLANES, lp)
    feat_major = lambda c: jnp.moveaxis(c, 2, -1).reshape(depth, n_pool, -1, LANES)
    ca_k, ca_v, ca_i, cb_k, cb_v = (feat_major(c) for c in (cache_a_k, cache_a_v, cache_a_kidx, cache_b_k, cache_b_v))

    def ffn_block(l, x):
        if l == 0:
            return _ffn(x, row(norm_ffn[l]), *ffn)
        return _moe_final(x, row(norm_ffn[l]), *moe, row(norm_final))

    def rows_of(pr, b, t):
        def tok_major(a, *s):
            c = a.shape[1]
            return jnp.moveaxis(jnp.moveaxis(a, 1, 0).reshape(c, b, t), 0, -1).reshape(b, t, *s)
        return (tok_major(pr["kat"], N_HEADS_A, HEAD_DIM), tok_major(pr["vat"], N_HEADS_A, HEAD_DIM),
                tok_major(pr["kwt"][:, :IDX_DIM], IDX_DIM), tok_major(pr["kbt"], N_HEADS_B, HEAD_DIM),
                tok_major(pr["vbt"], N_HEADS_B, HEAD_DIM),
                pr["u"].reshape(b, t, D_C)[:, t - (CONV_WIDTH - 1):])

    x = x_prompt.reshape(bsz * seq, D_MODEL)
    rows_p = []
    zeros_state = jnp.zeros((bsz, SUBLANES, D_C), F32)
    for l in range(depth):
        pr = _proj(x, row(norm_mix[l]), lw[l]["w_tok"], lw[l]["w_feat"], bsz, seq)
        rows_p.append(rows_of(pr, bsz, seq))
        oa = _dsa_prompt(pr, tiles_a, bsz, seq)
        ob = _diff_prompt(pr, tiles_b, lw[l]["lam"], lw[l]["g_col"], lw[l]["out_scale"], bsz, seq)
        x = _outproj(x, oa, ob, pr["u"], pr["gb"], lw[l]["cw"], lw[l]["w_out"], pr["u"], zeros_state,
                     seq, False)
        x = ffn_block(l, x)
    y_prompt = x.reshape(bsz, seq, D_MODEL)

    x = x_sample.reshape(n_s, D_MODEL)
    rows_s = []
    for l in range(depth):
        pr = _proj(x, row(norm_mix[l]), lw[l]["w_tok"], lw[l]["w_feat"], 1, n_s)
        rows_s.append(rows_of(pr, bd, n_new))
        r3 = lambda a: a.reshape(bd, n_new, a.shape[-1])
        new_rows = lambda a: _pad_rows(r3(a), LANES)
        new_rows_t = lambda a: _pad_rows(jnp.moveaxis(a[0].reshape(-1, bd, n_new), 0, -1), LANES).astype(BF16)
        qi_s = _pad_rows(r3(pr["qi"]).reshape(bd, n_new, IDX_HEADS, IDX_DIM).transpose(0, 2, 1, 3),
                         SUBLANES).reshape(bd, IDX_HEADS * SUBLANES, IDX_DIM)
        w_tok = jnp.moveaxis(pr["kwt"][0, IDX_DIM:IDX_DIM + IDX_HEADS].reshape(IDX_HEADS, bd, n_new), 0, -1)
        w_s = jnp.pad(_pad_rows(w_tok, SUBLANES) * IDX_HEADS ** -0.5, ((0, 0), (0, 0), (0, LANES - IDX_HEADS)))
        mb = _idx_sample(page_table, qi_s, w_s, new_rows(pr["kib"]), ca_i, l, past, n_new, N_HEADS_A)
        oa = _attn_sample(page_table, _block_diag_queries(r3(pr["qa"]), 1), mb, sbias_a,
                          new_rows(pr["kab"]), new_rows_t(pr["vatb"]), ca_k, ca_v, l, past)
        ob = _attn_sample(page_table, _block_diag_queries(r3(pr["qb"]), 2), None, sbias_b,
                          new_rows(pr["kbb"]), new_rows_t(pr["vbtb"]), cb_k, cb_v, l, past,
                          lam=lw[l]["lam"], g_row=lw[l]["g_row"], diff_scale=lw[l]["out_scale"])
        flat = lambda o: o[:, :n_new].reshape(n_s, o.shape[-1]).astype(BF16)
        st = state_conv[l]
        zero = jnp.zeros((bd, n_new - 1, D_C), F32)
        a = jnp.concatenate([st[:, 1:2], zero], axis=1).reshape(n_s, D_C)
        b = jnp.concatenate([st, zero[:, 1:]], axis=1).reshape(n_s, D_C)
        x = _outproj(x, flat(oa), flat(ob), pr["u"], pr["gb"], lw[l]["cw"], lw[l]["w_out"], a, b,
                     n_new, True)
        x = ffn_block(l, x)
    y_sample = x.reshape(bd, n_new, D_MODEL)

    stack = lambda rows: tuple(jnp.stack([r[i] for r in rows]) for i in range(6))
    return (y_prompt, y_sample) + stack(rows_p) + stack(rows_s)
```
